```python
import jax, jax.numpy as jnp
from jax import lax
import numpy as np

D_MODEL = 2048
BATCH = 2
SEQ = 8192
DEPTH = 1

PLE_DIM = 256
NORM_EPS = 1e-6
ATT_HEADS = 8
ATT_HEAD_DIM = 128
ATT_WIDTH = ATT_HEADS * ATT_HEAD_DIM
MOBA_BLOCK = 256
MOBA_TOPK = 3
MOBA_Q_CHUNK = 64
SGU_GROUPS = 8
SGU_GROUP_DIM = 128
SGU_WIDTH = SGU_GROUPS * SGU_GROUP_DIM
SGU_CHUNK = 128
PEER_HEADS = 8
PEER_N_KEYS = 128
PEER_N_EXPERTS = PEER_N_KEYS * PEER_N_KEYS
PEER_D_KEY = 256
PEER_HALF = PEER_D_KEY // 2
PEER_TOPK = 16
PEER_TOKEN_CHUNK = 128
IN_WIDTHS = (ATT_WIDTH, ATT_WIDTH, ATT_WIDTH, SGU_WIDTH, SGU_WIDTH, D_MODEL, D_MODEL)
IN_SPLITS = tuple(int(s) for s in np.cumsum(IN_WIDTHS)[:-1])
IN_TOTAL = int(sum(IN_WIDTHS))

kernel_name = "hybrid_moba_gmlp_peer_ple"


def rms_norm(x, gain):
    xf = x.astype(jnp.float32)
    y = xf * lax.rsqrt(jnp.mean(xf * xf, axis=-1, keepdims=True) + NORM_EPS)
    return (y * gain.astype(jnp.float32)).astype(x.dtype)


def layer_norm(x, gain, bias):
    xf = x.astype(jnp.float32)
    mu = jnp.mean(xf, axis=-1, keepdims=True)
    xc = xf - mu
    y = xc * lax.rsqrt(jnp.mean(xc * xc, axis=-1, keepdims=True) + NORM_EPS)
    return (y * gain.astype(jnp.float32) + bias.astype(jnp.float32)).astype(x.dtype)


def moba_attention(q, k, v):
    b, s, h, dh = q.shape
    nb = -(-s // MOBA_BLOCK)
    s_pad = nb * MOBA_BLOCK
    pad = ((0, 0), (0, s_pad - s), (0, 0), (0, 0))
    q, k, v = [jnp.pad(t, pad).transpose(0, 2, 1, 3) for t in (q, k, v)]
    kb = k.reshape(b, h, nb, MOBA_BLOCK, dh)
    vb = v.reshape(b, h, nb, MOBA_BLOCK, dh)
    k_mean = jnp.mean(kb.astype(jnp.float32), axis=3)
    n_sel = min(MOBA_TOPK, nb)
    n_chunks = s_pad // MOBA_Q_CHUNK
    scale = dh ** -0.5
    bi = jnp.arange(b)[:, None, None, None]
    hi = jnp.arange(h)[None, :, None, None]
    sel_len = n_sel * MOBA_BLOCK

    def chunk_fn(c):
        q0 = c * MOBA_Q_CHUNK
        j = q0 // MOBA_BLOCK
        qc = lax.dynamic_slice_in_dim(q, q0, MOBA_Q_CHUNK, axis=2)
        gate = jnp.einsum('bhqd,bhnd->bhqn', qc.astype(jnp.float32), k_mean)
        gate = jnp.where(jnp.arange(nb) < j, gate, -jnp.inf)
        _, sel = lax.top_k(gate, n_sel)
        sel_valid = jnp.arange(n_sel) < j
        ks = kb[bi, hi, sel]
        vs = vb[bi, hi, sel]
        s_sel = jnp.einsum('bhqd,bhqnkd->bhqnk', qc, ks).astype(jnp.float32) * scale
        s_sel = jnp.where(sel_valid[:, None], s_sel, -jnp.inf).reshape(b, h, MOBA_Q_CHUNK, sel_len)
        k_own = lax.dynamic_slice_in_dim(kb, j, 1, axis=2)[:, :, 0]
        v_own = lax.dynamic_slice_in_dim(vb, j, 1, axis=2)[:, :, 0]
        s_own = jnp.einsum('bhqd,bhkd->bhqk', qc, k_own).astype(jnp.float32) * scale
        q_pos = q0 + jnp.arange(MOBA_Q_CHUNK)
        k_pos = j * MOBA_BLOCK + jnp.arange(MOBA_BLOCK)
        s_own = jnp.where(k_pos[None, :] <= q_pos[:, None], s_own, -jnp.inf)
        probs = jax.nn.softmax(jnp.concatenate([s_sel, s_own], axis=-1), axis=-1)
        p_sel = probs[..., :sel_len].reshape(b, h, MOBA_Q_CHUNK, n_sel, MOBA_BLOCK).astype(vb.dtype)
        p_own = probs[..., sel_len:].astype(vb.dtype)
        return (jnp.einsum('bhqnk,bhqnkd->bhqd', p_sel, vs)
                + jnp.einsum('bhqk,bhkd->bhqd', p_own, v_own))

    outs = lax.map(chunk_fn, jnp.arange(n_chunks))
    out = outs.transpose(1, 0, 3, 2, 4).reshape(b, s_pad, h, dh)
    return out[:, :s]


def spatial_gating(u, vg, ln_g, ln_b, w_s, b_s):
    b, s, _ = u.shape
    vg = layer_norm(vg, ln_g, ln_b)
    nc = s // SGU_CHUNK
    vr = vg.reshape(b, nc, SGU_CHUNK, SGU_GROUPS, SGU_GROUP_DIM)
    causal = jnp.tril(jnp.ones((SGU_CHUNK, SGU_CHUNK), dtype=bool))
    w = jnp.where(causal[None], w_s, jnp.zeros_like(w_s))
    mixed = jnp.einsum('gts,bnsgc->bntgc', w, vr) + b_s.T[:, :, None]
    return u * mixed.reshape(b, s, SGU_WIDTH)


def peer(x, w_query, sub_keys, expert_down, expert_up):
    b, s, d = x.shape
    t = b * s
    xf = x.reshape(t, d)
    q = (xf @ w_query).reshape(t, PEER_HEADS, 2, PEER_HALF)
    scores = jnp.einsum('thpk,hpnk->thpn', q, sub_keys).astype(jnp.float32)
    top_s, top_i = lax.top_k(scores, PEER_TOPK)
    cand = top_s[:, :, 0, :, None] + top_s[:, :, 1, None, :]
    cand_id = top_i[:, :, 0, :, None] * PEER_N_KEYS + top_i[:, :, 1, None, :]
    n_cand = PEER_TOPK * PEER_TOPK
    best_s, best_pos = lax.top_k(cand.reshape(t, PEER_HEADS, n_cand), PEER_TOPK)
    expert_id = jnp.take_along_axis(cand_id.reshape(t, PEER_HEADS, n_cand), best_pos, axis=-1)
    gates = jax.nn.softmax(best_s, axis=-1).astype(x.dtype)
    n_e = PEER_HEADS * PEER_TOPK
    nch = t // PEER_TOKEN_CHUNK

    def chunk_fn(args):
        xc, idc, gc = args
        u = expert_down[idc]
        v = expert_up[idc]
        hid = jax.nn.gelu(jnp.einsum('cd,ced->ce', xc, u), approximate=False)
        return jnp.einsum('ce,ced->cd', gc * hid, v)

    y = lax.map(chunk_fn, (xf.reshape(nch, PEER_TOKEN_CHUNK, d),
                           expert_id.reshape(nch, PEER_TOKEN_CHUNK, n_e),
                           gates.reshape(nch, PEER_TOKEN_CHUNK, n_e)))
    return y.reshape(b, s, d)


def setup_inputs(seed: int = 0) -> dict:
    key = jax.random.key(seed)
    ks = jax.random.split(key, 24)
    f32 = jnp.float32
    L, D = DEPTH, D_MODEL

    def nrm(k, shape, scale):
        return jax.random.normal(k, shape, f32) * scale

    def gain(k, shape):
        return 1.0 + 0.02 * jax.random.normal(k, shape, f32)

    return {
        "x": jax.random.normal(ks[0], (BATCH, SEQ, D), f32),
        "p": jax.random.normal(ks[1], (DEPTH, BATCH, SEQ, PLE_DIM), f32),
        "norm_mix_g": gain(ks[2], (L, D)),
        "w_in": nrm(ks[3], (L, D, IN_TOTAL), D ** -0.5),
        "sgu_ln_g": gain(ks[4], (L, SGU_WIDTH)),
        "sgu_ln_b": nrm(ks[5], (L, SGU_WIDTH), 0.02),
        "sgu_w": nrm(ks[6], (L, SGU_GROUPS, SGU_CHUNK, SGU_CHUNK), SGU_CHUNK ** -0.5),
        "sgu_b": gain(ks[7], (L, SGU_GROUPS, SGU_CHUNK)),
        "w_branch_attn": nrm(ks[8], (L, ATT_WIDTH, D), ATT_WIDTH ** -0.5),
        "w_branch_sgu": nrm(ks[9], (L, SGU_WIDTH, D), SGU_WIDTH ** -0.5),
        "w_out": nrm(ks[10], (L, D, D), D ** -0.5),
        "norm_ffn_g": gain(ks[11], (L, D)),
        "peer_w_query": nrm(ks[12], (L, D, PEER_HEADS * PEER_D_KEY), D ** -0.5),
        "peer_sub_keys": nrm(ks[13], (L, PEER_HEADS, 2, PEER_N_KEYS, PEER_HALF), PEER_HALF ** -0.5),
        "peer_down": nrm(ks[14], (L, PEER_N_EXPERTS, D), D ** -0.5),
        "peer_up": nrm(ks[15], (L, PEER_N_EXPERTS, D), PEER_HEADS ** -0.5),
        "norm_ple_g": gain(ks[16], (L, D)),
        "ple_w_proj": nrm(ks[17], (L, PLE_DIM, D), PLE_DIM ** -0.5),
        "ple_w_gate": nrm(ks[18], (L, D, D), D ** -0.5),
        "final_norm_g": gain(ks[19], (D,)),
    }


def reference(x, p, norm_mix_g, w_in, sgu_ln_g, sgu_ln_b, sgu_w, sgu_b, w_branch_attn, w_branch_sgu,
              w_out, norm_ffn_g, peer_w_query, peer_sub_keys, peer_down, peer_up, norm_ple_g,
              ple_w_proj, ple_w_gate, final_norm_g):
    b, s, _ = x.shape
    h = x
    for i in range(DEPTH):
        xn = rms_norm(h, norm_mix_g[i])
        proj = xn @ w_in[i]
        q, k, v, u, vg, g_a, g_b = jnp.split(proj, IN_SPLITS, axis=-1)
        q = q.reshape(b, s, ATT_HEADS, ATT_HEAD_DIM)
        k = k.reshape(b, s, ATT_HEADS, ATT_HEAD_DIM)
        v = v.reshape(b, s, ATT_HEADS, ATT_HEAD_DIM)
        y_att = moba_attention(q, k, v).reshape(b, s, ATT_WIDTH)
        y_sgu = spatial_gating(jax.nn.gelu(u, approximate=False), jax.nn.gelu(vg, approximate=False),
                               sgu_ln_g[i], sgu_ln_b[i], sgu_w[i], sgu_b[i])
        merged = (jax.nn.sigmoid(g_a) * (y_att @ w_branch_attn[i])
                  + jax.nn.sigmoid(g_b) * (y_sgu @ w_branch_sgu[i]))
        h = h + merged @ w_out[i]
        h = h + peer(rms_norm(h, norm_ffn_g[i]), peer_w_query[i], peer_sub_keys[i], peer_down[i], peer_up[i])
        ple_gate = jax.nn.sigmoid(rms_norm(h, norm_ple_g[i]) @ ple_w_gate[i])
        h = h + ple_gate * (p[i] @ ple_w_proj[i])
    return rms_norm(h, final_norm_g)
```

```python
import functools

import jax
import jax.numpy as jnp
from jax import lax
from jax.experimental import pallas as pl
from jax.experimental.pallas import tpu as pltpu

F32 = jnp.float32
BF16 = jnp.bfloat16

NORM_EPS = 1e-6
ATT_HEADS = 8
ATT_HEAD_DIM = 128
ATT_WIDTH = ATT_HEADS * ATT_HEAD_DIM
MOBA_BLOCK = 256
MOBA_TOPK = 3
SGU_GROUPS = 8
SGU_GROUP_DIM = 128
SGU_WIDTH = SGU_GROUPS * SGU_GROUP_DIM
SGU_CHUNK = 128
PEER_HEADS = 8
PEER_N_KEYS = 128
PEER_HALF = 128
PEER_TOPK = 16

LANES = 128
VMEM_LIMIT = 56 * 1024 * 1024
MASK_NEG = -(2.0 ** 30)

COL_GATE_A = 0
COL_GATE_B = 2048
COL_Q = 4096
COL_K = COL_Q + ATT_WIDTH
COL_V = COL_K + ATT_WIDTH
COL_U = COL_V + ATT_WIDTH
COL_VG = COL_U + SGU_WIDTH


def _params(semantics):
    return pltpu.CompilerParams(dimension_semantics=semantics, vmem_limit_bytes=VMEM_LIMIT)


def _rms(x, gain):
    return x * lax.rsqrt(jnp.mean(x * x, axis=-1, keepdims=True) + NORM_EPS) * gain


def _gelu(x):
    return 0.5 * x * (1.0 + lax.erf(x * (0.5 ** 0.5)))


def _dot_nt(a, b):
    return lax.dot_general(a, b, (((1,), (1,)), ((), ())), preferred_element_type=F32)


def _inproj_kernel(x_ref, g_ref, w_ref, o_ref, xn_ref):
    @pl.when(pl.program_id(1) == 0)
    def _():
        xn_ref[...] = _rms(x_ref[...], g_ref[...]).astype(BF16)

    o_ref[...] = jnp.dot(xn_ref[...], w_ref[...], preferred_element_type=F32).astype(o_ref.dtype)


def _inproj(x, gain, w, tm=1024, tn=512):
    t, d = x.shape
    n = w.shape[1]
    return pl.pallas_call(
        _inproj_kernel,
        grid=(t // tm, n // tn),
        in_specs=[
            pl.BlockSpec((tm, d), lambda i, j: (i, 0)),
            pl.BlockSpec((1, d), lambda i, j: (0, 0)),
            pl.BlockSpec((d, tn), lambda i, j: (0, j)),
        ],
        out_specs=pl.BlockSpec((tm, tn), lambda i, j: (i, j)),
        out_shape=jax.ShapeDtypeStruct((t, n), BF16),
        scratch_shapes=[pltpu.VMEM((tm, d), BF16)],
        compiler_params=_params(("parallel", "arbitrary")),
        name="inproj",
    )(x, gain, w)


def _moba_kernel(q_ref, k_ref, v_ref, o_ref, kmean_ref):
    j = pl.program_id(2)
    blk = MOBA_BLOCK
    nb = k_ref.shape[0] // blk
    scale = ATT_HEAD_DIM ** -0.5

    @pl.when(j == 0)
    def _():
        kmean_ref[...] = jnp.zeros_like(kmean_ref)

        def body(n, c):
            kb = k_ref[pl.ds(pl.multiple_of(n * blk, blk), blk), :].astype(F32)
            kmean_ref[pl.ds(n, 1), :] = jnp.sum(kb, axis=0, keepdims=True) * (1.0 / blk)
            return c

        lax.fori_loop(0, nb, body, 0)

    q = q_ref[...]
    gate = _dot_nt(q, kmean_ref[...].astype(BF16))
    col = lax.broadcasted_iota(jnp.int32, gate.shape, 1)
    g = jnp.where(col < j, gate, -jnp.inf)
    maskneg = jnp.full(gate.shape, MASK_NEG, F32)
    for r in range(MOBA_TOPK):
        m = jnp.max(g, axis=1, keepdims=True)
        idx = jnp.min(jnp.where(g == m, col, LANES), axis=1, keepdims=True)
        idx = jnp.where(r < j, idx, -1)
        hit = col == idx
        maskneg = jnp.where(hit, 0.0, maskneg)
        g = jnp.where(hit, -jnp.inf, g)
    q_aug = jnp.concatenate([q, maskneg.astype(BF16)], axis=1)

    own = pl.ds(pl.multiple_of(j * blk, blk), blk)
    s = _dot_nt(q, k_ref[own, :]) * scale
    row = lax.broadcasted_iota(jnp.int32, s.shape, 0)
    kcol = lax.broadcasted_iota(jnp.int32, s.shape, 1)
    s = jnp.where(kcol <= row, s, -jnp.inf)
    m0 = jnp.max(s, axis=1, keepdims=True)
    p = jnp.exp(s - m0)
    l0 = jnp.sum(p, axis=1, keepdims=True)
    acc0 = jnp.dot(p.astype(BF16), v_ref[own, :], preferred_element_type=F32)

    blk_id = lax.broadcasted_iota(jnp.int32, (blk, LANES), 1)

    def body(n, carry):
        m_prev, l_prev, acc = carry
        rows = pl.ds(pl.multiple_of(n * blk, blk), blk)
        onehot = jnp.where(blk_id == n, 1.0, 0.0).astype(BF16)
        k_aug = jnp.concatenate([k_ref[rows, :], onehot], axis=1)
        s = _dot_nt(q_aug, k_aug) * scale
        m_new = jnp.maximum(m_prev, jnp.max(s, axis=1, keepdims=True))
        alpha = jnp.exp(m_prev - m_new)
        p = jnp.exp(s - m_new)
        l_new = alpha * l_prev + jnp.sum(p, axis=1, keepdims=True)
        acc = alpha * acc + jnp.dot(p.astype(BF16), v_ref[rows, :], preferred_element_type=F32)
        return m_new, l_new, acc

    _, l_fin, acc = lax.fori_loop(0, j, body, (m0, l0, acc0))
    o_ref[...] = (acc / l_fin).astype(o_ref.dtype)


def _moba(proj3):
    b, s, _ = proj3.shape
    blk, hd = MOBA_BLOCK, ATT_HEAD_DIM
    assert s % blk == 0 and s // blk <= LANES
    qb, kb, vb = COL_Q // hd, COL_K // hd, COL_V // hd
    return pl.pallas_call(
        _moba_kernel,
        grid=(b, ATT_HEADS, s // blk),
        in_specs=[
            pl.BlockSpec((None, blk, hd), lambda bi, h, j: (bi, j, qb + h)),
            pl.BlockSpec((None, s, hd), lambda bi, h, j: (bi, 0, kb + h)),
            pl.BlockSpec((None, s, hd), lambda bi, h, j: (bi, 0, vb + h)),
        ],
        out_specs=pl.BlockSpec((None, blk, hd), lambda bi, h, j: (bi, j, h)),
        out_shape=jax.ShapeDtypeStruct((b, s, ATT_WIDTH), BF16),
        scratch_shapes=[pltpu.VMEM((LANES, hd), F32)],
        compiler_params=_params(("parallel", "parallel", "arbitrary")),
        name="moba",
    )(proj3, proj3, proj3)


def _tail_kernel(ga_ref, gb_ref, u_ref, vg_ref, yatt_ref, x_ref, lng_ref, lnb_ref, ws_ref, bias_ref,
                 wba_ref, wbs_ref, wout_ref, gffn_ref, h_ref, hn_ref, ysgu_ref):
    tm = u_ref.shape[0]
    ch, gd = SGU_CHUNK, SGU_GROUP_DIM
    gu = _gelu(u_ref[...].astype(F32))
    gv = _gelu(vg_ref[...].astype(F32))
    mu = jnp.mean(gv, axis=-1, keepdims=True)
    xc = gv - mu
    vn = xc * lax.rsqrt(jnp.mean(xc * xc, axis=-1, keepdims=True) + NORM_EPS)
    vn = (vn * lng_ref[...] + lnb_ref[...]).astype(BF16)
    trow = lax.broadcasted_iota(jnp.int32, (ch, ch), 0)
    tcol = lax.broadcasted_iota(jnp.int32, (ch, ch), 1)
    for g in range(SGU_GROUPS):
        w = jnp.where(tcol <= trow, ws_ref[g], 0.0).astype(BF16)
        cols = slice(g * gd, (g + 1) * gd)
        for c in range(tm // ch):
            rows = slice(c * ch, (c + 1) * ch)
            mixed = jnp.dot(w, vn[rows, cols], preferred_element_type=F32) + bias_ref[:, cols]
            ysgu_ref[rows, cols] = (gu[rows, cols] * mixed).astype(BF16)
    a = jnp.dot(yatt_ref[...], wba_ref[...], preferred_element_type=F32)
    bsg = jnp.dot(ysgu_ref[...], wbs_ref[...], preferred_element_type=F32)
    merged = (jax.nn.sigmoid(ga_ref[...].astype(F32)) * a
              + jax.nn.sigmoid(gb_ref[...].astype(F32)) * bsg)
    h = x_ref[...] + jnp.dot(merged.astype(BF16), wout_ref[...], preferred_element_type=F32)
    h_ref[...] = h
    hn_ref[...] = _rms(h, gffn_ref[...]).astype(BF16)


def _const_spec(shape):
    return pl.BlockSpec(shape, lambda *_: (0,) * len(shape), pipeline_mode=pl.Buffered(1))


def _tail(proj, yatt, x, ln_g, ln_b, w_s, bias_full, w_ba, w_bs, w_out, g_ffn, tm=256):
    t, d = x.shape
    wide, narrow = d, SGU_WIDTH
    return pl.pallas_call(
        _tail_kernel,
        grid=(t // tm,),
        in_specs=[
            pl.BlockSpec((tm, wide), lambda i: (i, COL_GATE_A // wide)),
            pl.BlockSpec((tm, wide), lambda i: (i, COL_GATE_B // wide)),
            pl.BlockSpec((tm, narrow), lambda i: (i, COL_U // narrow)),
            pl.BlockSpec((tm, narrow), lambda i: (i, COL_VG // narrow)),
            pl.BlockSpec((tm, ATT_WIDTH), lambda i: (i, 0)),
            pl.BlockSpec((tm, d), lambda i: (i, 0)),
            _const_spec((1, narrow)),
            _const_spec((1, narrow)),
            _const_spec(w_s.shape),
            _const_spec(bias_full.shape),
            _const_spec(w_ba.shape),
            _const_spec(w_bs.shape),
            _const_spec(w_out.shape),
            _const_spec((1, d)),
        ],
        out_specs=[pl.BlockSpec((tm, d), lambda i: (i, 0)), pl.BlockSpec((tm, d), lambda i: (i, 0))],
        out_shape=[jax.ShapeDtypeStruct((t, d), F32), jax.ShapeDtypeStruct((t, d), BF16)],
        scratch_shapes=[pltpu.VMEM((tm, narrow), BF16)],
        compiler_params=_params(("parallel",)),
        name="mixer_tail",
    )(proj, proj, proj, proj, yatt, x, ln_g, ln_b, w_s, bias_full, w_ba, w_bs, w_out, g_ffn)


N_EXTRACT = PEER_TOPK + 1
CAND_PAIRS = tuple((a, b) for a in range(N_EXTRACT) for b in range(N_EXTRACT) if (a + 1) * (b + 1) <= N_EXTRACT)
CAND_ROWS = -(-len(CAND_PAIRS) // 8) * 8


def _extract_top(work, n_iter):
    rows = work.shape[0]
    row = lax.broadcasted_iota(jnp.int32, work.shape, 0)
    vals = []
    for _ in range(n_iter):
        m = jnp.max(work, axis=0, keepdims=True)
        idx = jnp.min(jnp.where(work == m, row, rows), axis=0, keepdims=True)
        work = jnp.where(row == idx, -jnp.inf, work)
        vals.append(m)
    return vals


def _route_kernel(hn_ref, wqt_ref, keys_ref, tq_ref, e0_ref, s1_ref, e1_ref, cand_ref):
    tm = hn_ref.shape[0]
    hn = hn_ref[...]
    cand_ref[...] = jnp.full(cand_ref.shape, -jnp.inf, F32)

    def head(hh, carry):
        w = wqt_ref[pl.ds(pl.multiple_of(hh * 2 * PEER_HALF, 2 * PEER_HALF), 2 * PEER_HALF), :]
        qt = _dot_nt(w, hn).astype(BF16)
        s0 = jnp.dot(keys_ref[2 * hh], qt[:PEER_HALF], preferred_element_type=F32)
        s1 = jnp.dot(keys_ref[2 * hh + 1], qt[PEER_HALF:], preferred_element_type=F32)
        tv0 = _extract_top(s0, N_EXTRACT)
        tv1 = _extract_top(s1, N_EXTRACT)
        for r, (a, b) in enumerate(CAND_PAIRS):
            cand_ref[r:r + 1, :] = tv0[a] + tv1[b]
        c = _extract_top(cand_ref[...], N_EXTRACT)
        z = jnp.ones_like(c[0])
        for r in range(1, PEER_TOPK):
            z = z + jnp.exp(c[r] - c[0])
        tau = 0.5 * (c[PEER_TOPK - 1] + c[PEER_TOPK])
        tq = tau - s0
        e0 = jnp.exp(s0 - tv0[0]) / z
        e1 = jnp.exp(s1 - tv1[0])
        for cc in range(tm // LANES):
            lanes = slice(cc * LANES, (cc + 1) * LANES)
            tq_ref[cc, hh] = tq[:, lanes]
            e0_ref[cc, hh] = e0[:, lanes]
            s1_ref[cc, hh] = s1[:, lanes]
            e1_ref[cc, hh] = e1[:, lanes]
        return carry

    lax.fori_loop(0, PEER_HEADS, head, 0)


def _route(hn, wq_t, keys, tm=256):
    t, d = hn.shape
    nck = tm // LANES
    out_block = pl.BlockSpec((nck, PEER_HEADS, PEER_N_KEYS, LANES), lambda i: (i, 0, 0, 0))
    out_shape = jax.ShapeDtypeStruct((t // LANES, PEER_HEADS, PEER_N_KEYS, LANES), F32)
    return pl.pallas_call(
        _route_kernel,
        grid=(t // tm,),
        in_specs=[
            pl.BlockSpec((tm, d), lambda i: (i, 0)),
            _const_spec(wq_t.shape),
            _const_spec(keys.shape),
        ],
        out_specs=[out_block] * 4,
        out_shape=[out_shape] * 4,
        scratch_shapes=[pltpu.VMEM((CAND_ROWS, tm), F32)],
        compiler_params=_params(("parallel",)),
        name="peer_route",
    )(hn, wq_t, keys)


def _experts_kernel(hn_ref, down_ref, upt_ref, tq_ref, e0_ref, s1_ref, e1_ref, o_ref, acc_ref, gh_ref):
    e = pl.program_id(1)
    te = down_ref.shape[0]
    tm = hn_ref.shape[0]

    @pl.when(e == 0)
    def _():
        acc_ref[...] = jnp.zeros_like(acc_ref)

    hid = _dot_nt(down_ref[...], hn_ref[...])
    for ii in range(te // PEER_N_KEYS):
        rows = slice(ii * PEER_N_KEYS, (ii + 1) * PEER_N_KEYS)
        for cc in range(tm // LANES):
            lanes = slice(cc * LANES, (cc + 1) * LANES)
            gate = jnp.zeros((PEER_N_KEYS, LANES), F32)
            for hh in range(PEER_HEADS):
                thr = tq_ref[cc, hh, ii:ii + 1, :]
                scale = e0_ref[cc, hh, ii:ii + 1, :]
                gate = gate + jnp.where(s1_ref[cc, hh] >= thr, e1_ref[cc, hh], 0.0) * scale
            gh_ref[rows, lanes] = (gate * _gelu(hid[rows, lanes])).astype(BF16)
    acc_ref[...] += jnp.dot(upt_ref[...], gh_ref[...], preferred_element_type=F32)

    @pl.when(e == pl.num_programs(1) - 1)
    def _():
        o_ref[...] = acc_ref[...].T


def _experts(hn, down, up_t, tq, e0, s1, e1, tm=512, te=1024):
    t, d = hn.shape
    n_exp = down.shape[0]
    nck = tm // LANES
    ni = te // PEER_N_KEYS
    assert ni % 8 == 0
    row_block = pl.BlockSpec((nck, PEER_HEADS, ni, LANES), lambda i, e: (i, 0, e, 0))
    full_block = pl.BlockSpec((nck, PEER_HEADS, PEER_N_KEYS, LANES), lambda i, e: (i, 0, 0, 0))
    return pl.pallas_call(
        _experts_kernel,
        grid=(t // tm, n_exp // te),
        in_specs=[
            pl.BlockSpec((tm, d), lambda i, e: (i, 0)),
            pl.BlockSpec((te, d), lambda i, e: (e, 0)),
            pl.BlockSpec((d, te), lambda i, e: (0, e)),
            row_block, row_block, full_block, full_block,
        ],
        out_specs=pl.BlockSpec((tm, d), lambda i, e: (i, 0)),
        out_shape=jax.ShapeDtypeStruct((t, d), F32),
        scratch_shapes=[pltpu.VMEM((d, tm), F32), pltpu.VMEM((te, tm), BF16)],
        compiler_params=_params(("parallel", "arbitrary")),
        name="peer_experts",
    )(hn, down, up_t, tq, e0, s1, e1)


def _ple_kernel(h_ref, y_ref, p_ref, gple_ref, wgate_ref, wproj_ref, gfin_ref, o_ref, *, final):
    h = h_ref[...] + y_ref[...]
    gate = jax.nn.sigmoid(jnp.dot(_rms(h, gple_ref[...]).astype(BF16), wgate_ref[...], preferred_element_type=F32))
    emb = jnp.dot(p_ref[...].astype(BF16), wproj_ref[...], preferred_element_type=F32)
    h = h + gate * emb
    o_ref[...] = _rms(h, gfin_ref[...]) if final else h


def _ple(h, y, p, g_ple, w_gate, w_proj, g_final, final, tm=512):
    t, d = h.shape
    pd = p.shape[1]
    return pl.pallas_call(
        functools.partial(_ple_kernel, final=final),
        grid=(t // tm,),
        in_specs=[
            pl.BlockSpec((tm, d), lambda i: (i, 0)),
            pl.BlockSpec((tm, d), lambda i: (i, 0)),
            pl.BlockSpec((tm, pd), lambda i: (i, 0)),
            _const_spec((1, d)),
            _const_spec(w_gate.shape),
            _const_spec(w_proj.shape),
            _const_spec((1, d)),
        ],
        out_specs=pl.BlockSpec((tm, d), lambda i: (i, 0)),
        out_shape=jax.ShapeDtypeStruct((t, d), F32),
        compiler_params=_params(("parallel",)),
        name="ple",
    )(h, y, p, g_ple, w_gate, w_proj, g_final)


def kernel(x, p, norm_mix_g, w_in, sgu_ln_g, sgu_ln_b, sgu_w, sgu_b, w_branch_attn, w_branch_sgu, w_out, norm_ffn_g, peer_w_query, peer_sub_keys, peer_down, peer_up, norm_ple_g, ple_w_proj, ple_w_gate, final_norm_g):
    b, s, d = x.shape
    t = b * s
    depth = w_in.shape[0]
    h = x.reshape(t, d)
    row = lambda v: v.reshape(1, -1)
    aw = ATT_WIDTH
    for i in range(depth):
        wi = w_in[i]
        w_perm = jnp.concatenate([wi[:, 3 * aw + 2 * SGU_WIDTH:], wi[:, :3 * aw + 2 * SGU_WIDTH]], axis=1).astype(BF16)
        proj = _inproj(h, row(norm_mix_g[i]), w_perm)
        yatt = _moba(proj.reshape(b, s, -1)).reshape(t, aw)
        bias_full = jnp.repeat(sgu_b[i].T, SGU_GROUP_DIM, axis=1)
        h, hn = _tail(proj, yatt, h, row(sgu_ln_g[i]), row(sgu_ln_b[i]), sgu_w[i], bias_full,
                      w_branch_attn[i].astype(BF16), w_branch_sgu[i].astype(BF16), w_out[i].astype(BF16),
                      row(norm_ffn_g[i]))
        keys = peer_sub_keys[i].reshape(2 * PEER_HEADS, PEER_N_KEYS, PEER_HALF).astype(BF16)
        tq, e0, s1, e1 = _route(hn, peer_w_query[i].T.astype(BF16), keys)
        y = _experts(hn, peer_down[i].astype(BF16), peer_up[i].T.astype(BF16), tq, e0, s1, e1)
        h = _ple(h, y, p[i].reshape(t, -1), row(norm_ple_g[i]), ple_w_gate[i].astype(BF16),
                 ple_w_proj[i].astype(BF16), row(final_norm_g), final=(i == depth - 1))
    if depth == 0:
        raise ValueError("depth must be at least 1")
    return h.reshape(b, s, d)
```

```python
import functools

import jax
import jax.numpy as jnp
from jax import lax
from jax.experimental import pallas as pl
from jax.experimental.pallas import tpu as pltpu

F32 = jnp.float32
BF16 = jnp.bfloat16

NORM_EPS = 1e-6
ATT_HEADS = 8
ATT_HEAD_DIM = 128
ATT_WIDTH = ATT_HEADS * ATT_HEAD_DIM
MOBA_BLOCK = 256
MOBA_TOPK = 3
MOBA_BLOCKS_PER_STEP = 4
SGU_GROUPS = 8
SGU_GROUP_DIM = 128
SGU_WIDTH = SGU_GROUPS * SGU_GROUP_DIM
SGU_CHUNK = 128
PEER_HEADS = 8
PEER_N_KEYS = 128
PEER_HALF = 128
PEER_TOPK = 16

LANES = 128
VMEM_LIMIT = 56 * 1024 * 1024
MASK_NEG = -(2.0 ** 30)
LOG2E = 1.4426950408889634

COL_GATE_A = 0
COL_GATE_B = 2048
COL_Q = 4096
COL_K = COL_Q + ATT_WIDTH
COL_V = COL_K + ATT_WIDTH
COL_U = COL_V + ATT_WIDTH
COL_VG = COL_U + SGU_WIDTH


def _params(semantics):
    return pltpu.CompilerParams(dimension_semantics=semantics, vmem_limit_bytes=VMEM_LIMIT)


def _rms(x, gain):
    return x * lax.rsqrt(jnp.mean(x * x, axis=-1, keepdims=True) + NORM_EPS) * gain


def _gelu(x):
    return 0.5 * x * (1.0 + lax.erf(x * (0.5 ** 0.5)))


def _dot_nt(a, b):
    return lax.dot_general(a, b, (((1,), (1,)), ((), ())), preferred_element_type=F32)


def _inproj_kernel(x_ref, g_ref, w_ref, o_ref, xn_ref):
    @pl.when(pl.program_id(1) == 0)
    def _():
        xn_ref[...] = _rms(x_ref[...], g_ref[...]).astype(BF16)

    o_ref[...] = jnp.dot(xn_ref[...], w_ref[...], preferred_element_type=F32).astype(o_ref.dtype)


def _inproj(x, gain, w, tm=1024, tn=512):
    t, d = x.shape
    n = w.shape[1]
    return pl.pallas_call(
        _inproj_kernel,
        grid=(t // tm, n // tn),
        in_specs=[
            pl.BlockSpec((tm, d), lambda i, j: (i, 0)),
            pl.BlockSpec((1, d), lambda i, j: (0, 0)),
            pl.BlockSpec((d, tn), lambda i, j: (0, j)),
        ],
        out_specs=pl.BlockSpec((tm, tn), lambda i, j: (i, j)),
        out_shape=jax.ShapeDtypeStruct((t, n), BF16),
        scratch_shapes=[pltpu.VMEM((tm, d), BF16)],
        compiler_params=_params(("parallel", "arbitrary")),
        name="inproj",
    )(x, gain, w)


def _moba_kernel(q_ref, k_ref, v_ref, o_ref, kmean_ref, vt_ref, mask_ref):
    j = pl.program_id(2)
    blk = MOBA_BLOCK
    nb = k_ref.shape[0] // blk
    nbp = kmean_ref.shape[0]
    c_exp = (ATT_HEAD_DIM ** -0.5) * LOG2E

    @pl.when(j == 0)
    def _():
        kmean_ref[...] = jnp.zeros_like(kmean_ref)

        def body(n, c):
            rows = pl.ds(pl.multiple_of(n * blk, blk), blk)
            kmean_ref[pl.ds(n, 1), :] = jnp.sum(k_ref[rows, :].astype(F32), axis=0, keepdims=True) * (1.0 / blk)
            vt_ref[n] = v_ref[rows, :].T
            return c

        lax.fori_loop(0, nb, body, 0)

    q = q_ref[...]
    gate = _dot_nt(kmean_ref[...].astype(BF16), q)
    row = lax.broadcasted_iota(jnp.int32, gate.shape, 0)
    g = jnp.where(row < j, gate, -jnp.inf)
    maskneg = jnp.full(gate.shape, MASK_NEG, F32)
    for r in range(MOBA_TOPK):
        m = jnp.max(g, axis=0, keepdims=True)
        idx = jnp.min(jnp.where(g == m, row, nbp), axis=0, keepdims=True)
        idx = jnp.where(r < j, idx, -1)
        hit = row == idx
        maskneg = jnp.where(hit, 0.0, maskneg)
        g = jnp.where(hit, -jnp.inf, g)
    mask_ref[...] = maskneg

    own = pl.ds(pl.multiple_of(j * blk, blk), blk)
    t = _dot_nt(k_ref[own, :], q) * c_exp
    krow = lax.broadcasted_iota(jnp.int32, t.shape, 0)
    qcol = lax.broadcasted_iota(jnp.int32, t.shape, 1)
    t = jnp.where(krow <= qcol, t, -jnp.inf)
    m0 = jnp.max(t, axis=0, keepdims=True)
    p = jnp.exp2(t - m0)
    l0 = jnp.sum(p, axis=0, keepdims=True)
    acc0 = jnp.dot(vt_ref[j], p.astype(BF16), preferred_element_type=F32)

    def attend(blocks, carry):
        m_prev, l_prev, acc = carry
        ts = []
        for n in blocks:
            rows = pl.ds(pl.multiple_of(n * blk, blk), blk)
            ts.append(_dot_nt(k_ref[rows, :], q) * c_exp + mask_ref[pl.ds(n, 1), :])
        m_new = m_prev
        for t in ts:
            m_new = jnp.maximum(m_new, jnp.max(t, axis=0, keepdims=True))
        alpha = jnp.exp2(m_prev - m_new)
        l_new = alpha * l_prev
        acc = alpha * acc
        for n, t in zip(blocks, ts):
            p = jnp.exp2(t - m_new)
            l_new = l_new + jnp.sum(p, axis=0, keepdims=True)
            acc = acc + jnp.dot(vt_ref[n], p.astype(BF16), preferred_element_type=F32)
        return m_new, l_new, acc

    unroll = MOBA_BLOCKS_PER_STEP
    carry = lax.fori_loop(0, j // unroll, lambda i, c: attend([i * unroll + u for u in range(unroll)], c),
                          (m0, l0, acc0))
    _, l_fin, acc = lax.fori_loop((j // unroll) * unroll, j, lambda n, c: attend([n], c), carry)
    o_ref[...] = (acc / l_fin).T.astype(o_ref.dtype)


def _moba(proj3):
    b, s, _ = proj3.shape
    blk, hd = MOBA_BLOCK, ATT_HEAD_DIM
    assert s % blk == 0
    nbp = -(-(s // blk) // 16) * 16
    qb, kb, vb = COL_Q // hd, COL_K // hd, COL_V // hd
    return pl.pallas_call(
        _moba_kernel,
        grid=(b, ATT_HEADS, s // blk),
        in_specs=[
            pl.BlockSpec((None, blk, hd), lambda bi, h, j: (bi, j, qb + h)),
            pl.BlockSpec((None, s, hd), lambda bi, h, j: (bi, 0, kb + h)),
            pl.BlockSpec((None, s, hd), lambda bi, h, j: (bi, 0, vb + h)),
        ],
        out_specs=pl.BlockSpec((None, blk, hd), lambda bi, h, j: (bi, j, h)),
        out_shape=jax.ShapeDtypeStruct((b, s, ATT_WIDTH), BF16),
        scratch_shapes=[
            pltpu.VMEM((nbp, hd), F32),
            pltpu.VMEM((s // blk, hd, blk), BF16),
            pltpu.VMEM((nbp, blk), F32),
        ],
        compiler_params=_params(("parallel", "parallel", "arbitrary")),
        name="moba",
    )(proj3, proj3, proj3)


def _tail_kernel(ga_ref, gb_ref, u_ref, vg_ref, yatt_ref, x_ref, lng_ref, lnb_ref, ws_ref, bias_ref,
                 wba_ref, wbs_ref, wout_ref, gffn_ref, h_ref, hn_ref, ysgu_ref):
    tm = u_ref.shape[0]
    ch, gd = SGU_CHUNK, SGU_GROUP_DIM
    gu = _gelu(u_ref[...].astype(F32))
    gv = _gelu(vg_ref[...].astype(F32))
    mu = jnp.mean(gv, axis=-1, keepdims=True)
    xc = gv - mu
    vn = xc * lax.rsqrt(jnp.mean(xc * xc, axis=-1, keepdims=True) + NORM_EPS)
    vn = (vn * lng_ref[...] + lnb_ref[...]).astype(BF16)
    trow = lax.broadcasted_iota(jnp.int32, (ch, ch), 0)
    tcol = lax.broadcasted_iota(jnp.int32, (ch, ch), 1)
    for g in range(SGU_GROUPS):
        w = jnp.where(tcol <= trow, ws_ref[g], 0.0).astype(BF16)
        cols = slice(g * gd, (g + 1) * gd)
        for c in range(tm // ch):
            rows = slice(c * ch, (c + 1) * ch)
            mixed = jnp.dot(w, vn[rows, cols], preferred_element_type=F32) + bias_ref[:, cols]
            ysgu_ref[rows, cols] = (gu[rows, cols] * mixed).astype(BF16)
    a = jnp.dot(yatt_ref[...], wba_ref[...], preferred_element_type=F32)
    bsg = jnp.dot(ysgu_ref[...], wbs_ref[...], preferred_element_type=F32)
    merged = (jax.nn.sigmoid(ga_ref[...].astype(F32)) * a
              + jax.nn.sigmoid(gb_ref[...].astype(F32)) * bsg)
    h = x_ref[...] + jnp.dot(merged.astype(BF16), wout_ref[...], preferred_element_type=F32)
    h_ref[...] = h
    hn_ref[...] = _rms(h, gffn_ref[...]).astype(BF16)


def _const_spec(shape):
    return pl.BlockSpec(shape, lambda *_: (0,) * len(shape), pipeline_mode=pl.Buffered(1))


def _tail(proj, yatt, x, ln_g, ln_b, w_s, bias_full, w_ba, w_bs, w_out, g_ffn, tm=256):
    t, d = x.shape
    wide, narrow = d, SGU_WIDTH
    return pl.pallas_call(
        _tail_kernel,
        grid=(t // tm,),
        in_specs=[
            pl.BlockSpec((tm, wide), lambda i: (i, COL_GATE_A // wide)),
            pl.BlockSpec((tm, wide), lambda i: (i, COL_GATE_B // wide)),
            pl.BlockSpec((tm, narrow), lambda i: (i, COL_U // narrow)),
            pl.BlockSpec((tm, narrow), lambda i: (i, COL_VG // narrow)),
            pl.BlockSpec((tm, ATT_WIDTH), lambda i: (i, 0)),
            pl.BlockSpec((tm, d), lambda i: (i, 0)),
            _const_spec((1, narrow)),
            _const_spec((1, narrow)),
            _const_spec(w_s.shape),
            _const_spec(bias_full.shape),
            _const_spec(w_ba.shape),
            _const_spec(w_bs.shape),
            _const_spec(w_out.shape),
            _const_spec((1, d)),
        ],
        out_specs=[pl.BlockSpec((tm, d), lambda i: (i, 0)), pl.BlockSpec((tm, d), lambda i: (i, 0))],
        out_shape=[jax.ShapeDtypeStruct((t, d), F32), jax.ShapeDtypeStruct((t, d), BF16)],
        scratch_shapes=[pltpu.VMEM((tm, narrow), BF16)],
        compiler_params=_params(("parallel",)),
        name="mixer_tail",
    )(proj, proj, proj, proj, yatt, x, ln_g, ln_b, w_s, bias_full, w_ba, w_bs, w_out, g_ffn)


N_EXTRACT = PEER_TOPK + 1
CAND_PAIRS = tuple((a, b) for a in range(N_EXTRACT) for b in range(N_EXTRACT) if (a + 1) * (b + 1) <= N_EXTRACT)
CAND_ROWS = -(-len(CAND_PAIRS) // 8) * 8


def _extract_top(work, n_iter):
    rows = work.shape[0]
    row = lax.broadcasted_iota(jnp.int32, work.shape, 0)
    vals = []
    for _ in range(n_iter):
        m = jnp.max(work, axis=0, keepdims=True)
        idx = jnp.min(jnp.where(work == m, row, rows), axis=0, keepdims=True)
        work = jnp.where(row == idx, -jnp.inf, work)
        vals.append(m)
    return vals


def _route_kernel(hn_ref, wqt_ref, keys_ref, tq_ref, e0_ref, s1_ref, e1_ref, cand_ref):
    tm = hn_ref.shape[0]
    hn = hn_ref[...]
    cand_ref[...] = jnp.full(cand_ref.shape, -jnp.inf, F32)

    def head(hh, carry):
        w = wqt_ref[pl.ds(pl.multiple_of(hh * 2 * PEER_HALF, 2 * PEER_HALF), 2 * PEER_HALF), :]
        qt = _dot_nt(w, hn).astype(BF16)
        s0 = jnp.dot(keys_ref[2 * hh], qt[:PEER_HALF], preferred_element_type=F32)
        s1 = jnp.dot(keys_ref[2 * hh + 1], qt[PEER_HALF:], preferred_element_type=F32)
        tv0 = _extract_top(s0, N_EXTRACT)
        tv1 = _extract_top(s1, N_EXTRACT)
        for r, (a, b) in enumerate(CAND_PAIRS):
            cand_ref[r:r + 1, :] = tv0[a] + tv1[b]
        c = _extract_top(cand_ref[...], N_EXTRACT)
        z = jnp.ones_like(c[0])
        for r in range(1, PEER_TOPK):
            z = z + jnp.exp(c[r] - c[0])
        tau = 0.5 * (c[PEER_TOPK - 1] + c[PEER_TOPK])
        tq = tau - s0
        e0 = jnp.exp(s0 - tv0[0]) / z
        e1 = jnp.exp(s1 - tv1[0])
        for cc in range(tm // LANES):
            lanes = slice(cc * LANES, (cc + 1) * LANES)
            tq_ref[cc, hh] = tq[:, lanes]
            e0_ref[cc, hh] = e0[:, lanes]
            s1_ref[cc, hh] = s1[:, lanes]
            e1_ref[cc, hh] = e1[:, lanes]
        return carry

    lax.fori_loop(0, PEER_HEADS, head, 0)


def _route(hn, wq_t, keys, tm=256):
    t, d = hn.shape
    nck = tm // LANES
    out_block = pl.BlockSpec((nck, PEER_HEADS, PEER_N_KEYS, LANES), lambda i: (i, 0, 0, 0))
    out_shape = jax.ShapeDtypeStruct((t // LANES, PEER_HEADS, PEER_N_KEYS, LANES), F32)
    return pl.pallas_call(
        _route_kernel,
        grid=(t // tm,),
        in_specs=[
            pl.BlockSpec((tm, d), lambda i: (i, 0)),
            _const_spec(wq_t.shape),
            _const_spec(keys.shape),
        ],
        out_specs=[out_block] * 4,
        out_shape=[out_shape] * 4,
        scratch_shapes=[pltpu.VMEM((CAND_ROWS, tm), F32)],
        compiler_params=_params(("parallel",)),
        name="peer_route",
    )(hn, wq_t, keys)


def _experts_kernel(hn_ref, down_ref, upt_ref, tq_ref, e0_ref, s1_ref, e1_ref, o_ref, acc_ref, gh_ref):
    e = pl.program_id(1)
    te = down_ref.shape[0]
    tm = hn_ref.shape[0]

    @pl.when(e == 0)
    def _():
        acc_ref[...] = jnp.zeros_like(acc_ref)

    hid = _dot_nt(down_ref[...], hn_ref[...])
    for ii in range(te // PEER_N_KEYS):
        rows = slice(ii * PEER_N_KEYS, (ii + 1) * PEER_N_KEYS)
        for cc in range(tm // LANES):
            lanes = slice(cc * LANES, (cc + 1) * LANES)
            gate = jnp.zeros((PEER_N_KEYS, LANES), F32)
            for hh in range(PEER_HEADS):
                thr = tq_ref[cc, hh, ii:ii + 1, :]
                scale = e0_ref[cc, hh, ii:ii + 1, :]
                gate = gate + jnp.where(s1_ref[cc, hh] >= thr, e1_ref[cc, hh], 0.0) * scale
            gh_ref[rows, lanes] = (gate * _gelu(hid[rows, lanes])).astype(BF16)
    acc_ref[...] += jnp.dot(upt_ref[...], gh_ref[...], preferred_element_type=F32)

    @pl.when(e == pl.num_programs(1) - 1)
    def _():
        o_ref[...] = acc_ref[...].T


def _experts(hn, down, up_t, tq, e0, s1, e1, tm=512, te=1024):
    t, d = hn.shape
    n_exp = down.shape[0]
    nck = tm // LANES
    ni = te // PEER_N_KEYS
    assert ni % 8 == 0
    row_block = pl.BlockSpec((nck, PEER_HEADS, ni, LANES), lambda i, e: (i, 0, e, 0))
    full_block = pl.BlockSpec((nck, PEER_HEADS, PEER_N_KEYS, LANES), lambda i, e: (i, 0, 0, 0))
    return pl.pallas_call(
        _experts_kernel,
        grid=(t // tm, n_exp // te),
        in_specs=[
            pl.BlockSpec((tm, d), lambda i, e: (i, 0)),
            pl.BlockSpec((te, d), lambda i, e: (e, 0)),
            pl.BlockSpec((d, te), lambda i, e: (0, e)),
            row_block, row_block, full_block, full_block,
        ],
        out_specs=pl.BlockSpec((tm, d), lambda i, e: (i, 0)),
        out_shape=jax.ShapeDtypeStruct((t, d), F32),
        scratch_shapes=[pltpu.VMEM((d, tm), F32), pltpu.VMEM((te, tm), BF16)],
        compiler_params=_params(("parallel", "arbitrary")),
        name="peer_experts",
    )(hn, down, up_t, tq, e0, s1, e1)


def _ple_kernel(h_ref, y_ref, p_ref, gple_ref, wgate_ref, wproj_ref, gfin_ref, o_ref, *, final):
    h = h_ref[...] + y_ref[...]
    gate = jax.nn.sigmoid(jnp.dot(_rms(h, gple_ref[...]).astype(BF16), wgate_ref[...], preferred_element_type=F32))
    emb = jnp.dot(p_ref[...].astype(BF16), wproj_ref[...], preferred_element_type=F32)
    h = h + gate * emb
    o_ref[...] = _rms(h, gfin_ref[...]) if final else h


def _ple(h, y, p, g_ple, w_gate, w_proj, g_final, final, tm=512):
    t, d = h.shape
    pd = p.shape[1]
    return pl.pallas_call(
        functools.partial(_ple_kernel, final=final),
        grid=(t // tm,),
        in_specs=[
            pl.BlockSpec((tm, d), lambda i: (i, 0)),
            pl.BlockSpec((tm, d), lambda i: (i, 0)),
            pl.BlockSpec((tm, pd), lambda i: (i, 0)),
            _const_spec((1, d)),
            _const_spec(w_gate.shape),
            _const_spec(w_proj.shape),
            _const_spec((1, d)),
        ],
        out_specs=pl.BlockSpec((tm, d), lambda i: (i, 0)),
        out_shape=jax.ShapeDtypeStruct((t, d), F32),
        compiler_params=_params(("parallel",)),
        name="ple",
    )(h, y, p, g_ple, w_gate, w_proj, g_final)


def kernel(x, p, norm_mix_g, w_in, sgu_ln_g, sgu_ln_b, sgu_w, sgu_b, w_branch_attn, w_branch_sgu, w_out, norm_ffn_g, peer_w_query, peer_sub_keys, peer_down, peer_up, norm_ple_g, ple_w_proj, ple_w_gate, final_norm_g):
    b, s, d = x.shape
    t = b * s
    depth = w_in.shape[0]
    h = x.reshape(t, d)
    row = lambda v: v.reshape(1, -1)
    aw = ATT_WIDTH
    for i in range(depth):
        wi = w_in[i]
        w_perm = jnp.concatenate([wi[:, 3 * aw + 2 * SGU_WIDTH:], wi[:, :3 * aw + 2 * SGU_WIDTH]], axis=1).astype(BF16)
        proj = _inproj(h, row(norm_mix_g[i]), w_perm)
        yatt = _moba(proj.reshape(b, s, -1)).reshape(t, aw)
        bias_full = jnp.repeat(sgu_b[i].T, SGU_GROUP_DIM, axis=1)
        h, hn = _tail(proj, yatt, h, row(sgu_ln_g[i]), row(sgu_ln_b[i]), sgu_w[i], bias_full,
                      w_branch_attn[i].astype(BF16), w_branch_sgu[i].astype(BF16), w_out[i].astype(BF16),
                      row(norm_ffn_g[i]))
        keys = peer_sub_keys[i].reshape(2 * PEER_HEADS, PEER_N_KEYS, PEER_HALF).astype(BF16)
        tq, e0, s1, e1 = _route(hn, peer_w_query[i].T.astype(BF16), keys)
        y = _experts(hn, peer_down[i].astype(BF16), peer_up[i].T.astype(BF16), tq, e0, s1, e1)
        h = _ple(h, y, p[i].reshape(t, -1), row(norm_ple_g[i]), ple_w_gate[i].astype(BF16),
                 ple_w_proj[i].astype(BF16), row(final_norm_g), final=(i == depth - 1))
    if depth == 0:
        raise ValueError("depth must be at least 1")
    return h.reshape(b, s, d)
```

```python
import functools

import jax
import jax.numpy as jnp
from jax import lax
from jax.experimental import pallas as pl
from jax.experimental.pallas import tpu as pltpu

F32 = jnp.float32
BF16 = jnp.bfloat16

NORM_EPS = 1e-6
ATT_HEADS = 8
ATT_HEAD_DIM = 128
ATT_WIDTH = ATT_HEADS * ATT_HEAD_DIM
MOBA_BLOCK = 256
MOBA_TOPK = 3
MOBA_BLOCKS_PER_STEP = 4
SGU_GROUPS = 8
SGU_GROUP_DIM = 128
SGU_WIDTH = SGU_GROUPS * SGU_GROUP_DIM
SGU_CHUNK = 128
PEER_HEADS = 8
PEER_N_KEYS = 128
PEER_HALF = 128
PEER_TOPK = 16

LANES = 128
VMEM_LIMIT = 56 * 1024 * 1024
MASK_NEG = -(2.0 ** 30)
LOG2E = 1.4426950408889634

COL_GATE_A = 0
COL_GATE_B = 2048
COL_Q = 4096
COL_K = COL_Q + ATT_WIDTH
COL_V = COL_K + ATT_WIDTH
COL_U = COL_V + ATT_WIDTH
COL_VG = COL_U + SGU_WIDTH


def _params(semantics, flags=None):
    return pltpu.CompilerParams(dimension_semantics=semantics, vmem_limit_bytes=VMEM_LIMIT, flags=flags)


def _rms(x, gain):
    return x * lax.rsqrt(jnp.mean(x * x, axis=-1, keepdims=True) + NORM_EPS) * gain


def _gelu(x):
    return 0.5 * x * (1.0 + lax.erf(x * (0.5 ** 0.5)))


def _dot_nt(a, b):
    return lax.dot_general(a, b, (((1,), (1,)), ((), ())), preferred_element_type=F32)


def _inproj_kernel(x_ref, g_ref, w_ref, o_ref, xn_ref):
    @pl.when(pl.program_id(1) == 0)
    def _():
        xn_ref[...] = _rms(x_ref[...], g_ref[...]).astype(BF16)

    o_ref[...] = jnp.dot(xn_ref[...], w_ref[...], preferred_element_type=F32).astype(o_ref.dtype)


def _inproj(x, gain, w, tm=1024, tn=512):
    t, d = x.shape
    n = w.shape[1]
    return pl.pallas_call(
        _inproj_kernel,
        grid=(t // tm, n // tn),
        in_specs=[
            pl.BlockSpec((tm, d), lambda i, j: (i, 0)),
            pl.BlockSpec((1, d), lambda i, j: (0, 0)),
            pl.BlockSpec((d, tn), lambda i, j: (0, j)),
        ],
        out_specs=pl.BlockSpec((tm, tn), lambda i, j: (i, j)),
        out_shape=jax.ShapeDtypeStruct((t, n), BF16),
        scratch_shapes=[pltpu.VMEM((tm, d), BF16)],
        compiler_params=_params(("parallel", "arbitrary")),
        name="inproj",
    )(x, gain, w)


def _moba_kernel(q_ref, k_ref, v_ref, o_ref, kmean_ref, vt_ref, mask_ref):
    j = pl.program_id(2)
    blk = MOBA_BLOCK
    nb = k_ref.shape[0] // blk
    nbp = kmean_ref.shape[0]
    c_exp = (ATT_HEAD_DIM ** -0.5) * LOG2E

    @pl.when(j == 0)
    def _():
        kmean_ref[...] = jnp.zeros_like(kmean_ref)

        def body(n, c):
            rows = pl.ds(pl.multiple_of(n * blk, blk), blk)
            kmean_ref[pl.ds(n, 1), :] = jnp.sum(k_ref[rows, :].astype(F32), axis=0, keepdims=True) * (1.0 / blk)
            vt_ref[n] = v_ref[rows, :].T
            return c

        lax.fori_loop(0, nb, body, 0)

    q = q_ref[...]
    gate = _dot_nt(kmean_ref[...].astype(BF16), q)
    row = lax.broadcasted_iota(jnp.int32, gate.shape, 0)
    g = jnp.where(row < j, gate, -jnp.inf)
    maskneg = jnp.full(gate.shape, MASK_NEG, F32)
    for r in range(MOBA_TOPK):
        m = jnp.max(g, axis=0, keepdims=True)
        idx = jnp.min(jnp.where(g == m, row, nbp), axis=0, keepdims=True)
        idx = jnp.where(r < j, idx, -1)
        hit = row == idx
        maskneg = jnp.where(hit, 0.0, maskneg)
        g = jnp.where(hit, -jnp.inf, g)
    mask_ref[...] = maskneg

    own = pl.ds(pl.multiple_of(j * blk, blk), blk)
    t = _dot_nt(k_ref[own, :], q) * c_exp
    krow = lax.broadcasted_iota(jnp.int32, t.shape, 0)
    qcol = lax.broadcasted_iota(jnp.int32, t.shape, 1)
    t = jnp.where(krow <= qcol, t, -jnp.inf)
    m0 = jnp.max(t, axis=0, keepdims=True)
    p = jnp.exp2(t - m0)
    l0 = jnp.sum(p, axis=0, keepdims=True)
    acc0 = jnp.dot(vt_ref[j], p.astype(BF16), preferred_element_type=F32)

    def attend(blocks, carry):
        m_prev, l_prev, acc = carry
        ts = []
        for n in blocks:
            rows = pl.ds(pl.multiple_of(n * blk, blk), blk)
            ts.append(_dot_nt(k_ref[rows, :], q) * c_exp + mask_ref[pl.ds(n, 1), :])
        m_new = m_prev
        for t in ts:
            m_new = jnp.maximum(m_new, jnp.max(t, axis=0, keepdims=True))
        alpha = jnp.exp2(m_prev - m_new)
        l_new = alpha * l_prev
        acc = alpha * acc
        for n, t in zip(blocks, ts):
            p = jnp.exp2(t - m_new)
            l_new = l_new + jnp.sum(p, axis=0, keepdims=True)
            acc = acc + jnp.dot(vt_ref[n], p.astype(BF16), preferred_element_type=F32)
        return m_new, l_new, acc

    unroll = MOBA_BLOCKS_PER_STEP
    carry = lax.fori_loop(0, j // unroll, lambda i, c: attend([i * unroll + u for u in range(unroll)], c),
                          (m0, l0, acc0))
    _, l_fin, acc = lax.fori_loop((j // unroll) * unroll, j, lambda n, c: attend([n], c), carry)
    o_ref[...] = (acc / l_fin).T.astype(o_ref.dtype)


def _moba(proj3):
    b, s, _ = proj3.shape
    blk, hd = MOBA_BLOCK, ATT_HEAD_DIM
    assert s % blk == 0
    nbp = -(-(s // blk) // 16) * 16
    qb, kb, vb = COL_Q // hd, COL_K // hd, COL_V // hd
    return pl.pallas_call(
        _moba_kernel,
        grid=(b, ATT_HEADS, s // blk),
        in_specs=[
            pl.BlockSpec((None, blk, hd), lambda bi, h, j: (bi, j, qb + h)),
            pl.BlockSpec((None, s, hd), lambda bi, h, j: (bi, 0, kb + h)),
            pl.BlockSpec((None, s, hd), lambda bi, h, j: (bi, 0, vb + h)),
        ],
        out_specs=pl.BlockSpec((None, blk, hd), lambda bi, h, j: (bi, j, h)),
        out_shape=jax.ShapeDtypeStruct((b, s, ATT_WIDTH), BF16),
        scratch_shapes=[
            pltpu.VMEM((nbp, hd), F32),
            pltpu.VMEM((s // blk, hd, blk), BF16),
            pltpu.VMEM((nbp, blk), F32),
        ],
        compiler_params=_params(("parallel", "parallel", "arbitrary")),
        name="moba",
    )(proj3, proj3, proj3)


def _tail_kernel(ga_ref, gb_ref, u_ref, vg_ref, yatt_ref, x_ref, lng_ref, lnb_ref, ws_ref, bias_ref,
                 wba_ref, wbs_ref, wout_ref, gffn_ref, h_ref, hn_ref, ysgu_ref):
    tm = u_ref.shape[0]
    ch, gd = SGU_CHUNK, SGU_GROUP_DIM
    gu = _gelu(u_ref[...].astype(F32))
    gv = _gelu(vg_ref[...].astype(F32))
    mu = jnp.mean(gv, axis=-1, keepdims=True)
    xc = gv - mu
    vn = xc * lax.rsqrt(jnp.mean(xc * xc, axis=-1, keepdims=True) + NORM_EPS)
    vn = (vn * lng_ref[...] + lnb_ref[...]).astype(BF16)
    trow = lax.broadcasted_iota(jnp.int32, (ch, ch), 0)
    tcol = lax.broadcasted_iota(jnp.int32, (ch, ch), 1)
    for g in range(SGU_GROUPS):
        w = jnp.where(tcol <= trow, ws_ref[g], 0.0).astype(BF16)
        cols = slice(g * gd, (g + 1) * gd)
        for c in range(tm // ch):
            rows = slice(c * ch, (c + 1) * ch)
            mixed = jnp.dot(w, vn[rows, cols], preferred_element_type=F32) + bias_ref[:, cols]
            ysgu_ref[rows, cols] = (gu[rows, cols] * mixed).astype(BF16)
    a = jnp.dot(yatt_ref[...], wba_ref[...], preferred_element_type=F32)
    bsg = jnp.dot(ysgu_ref[...], wbs_ref[...], preferred_element_type=F32)
    merged = (jax.nn.sigmoid(ga_ref[...].astype(F32)) * a
              + jax.nn.sigmoid(gb_ref[...].astype(F32)) * bsg)
    h = x_ref[...] + jnp.dot(merged.astype(BF16), wout_ref[...], preferred_element_type=F32)
    h_ref[...] = h
    hn_ref[...] = _rms(h, gffn_ref[...]).astype(BF16)


def _const_spec(shape):
    return pl.BlockSpec(shape, lambda *_: (0,) * len(shape), pipeline_mode=pl.Buffered(1))


def _tail(proj, yatt, x, ln_g, ln_b, w_s, bias_full, w_ba, w_bs, w_out, g_ffn, tm=256):
    t, d = x.shape
    wide, narrow = d, SGU_WIDTH
    return pl.pallas_call(
        _tail_kernel,
        grid=(t // tm,),
        in_specs=[
            pl.BlockSpec((tm, wide), lambda i: (i, COL_GATE_A // wide)),
            pl.BlockSpec((tm, wide), lambda i: (i, COL_GATE_B // wide)),
            pl.BlockSpec((tm, narrow), lambda i: (i, COL_U // narrow)),
            pl.BlockSpec((tm, narrow), lambda i: (i, COL_VG // narrow)),
            pl.BlockSpec((tm, ATT_WIDTH), lambda i: (i, 0)),
            pl.BlockSpec((tm, d), lambda i: (i, 0)),
            _const_spec((1, narrow)),
            _const_spec((1, narrow)),
            _const_spec(w_s.shape),
            _const_spec(bias_full.shape),
            _const_spec(w_ba.shape),
            _const_spec(w_bs.shape),
            _const_spec(w_out.shape),
            _const_spec((1, d)),
        ],
        out_specs=[pl.BlockSpec((tm, d), lambda i: (i, 0)), pl.BlockSpec((tm, d), lambda i: (i, 0))],
        out_shape=[jax.ShapeDtypeStruct((t, d), F32), jax.ShapeDtypeStruct((t, d), BF16)],
        scratch_shapes=[pltpu.VMEM((tm, narrow), BF16)],
        compiler_params=_params(("parallel",)),
        name="mixer_tail",
    )(proj, proj, proj, proj, yatt, x, ln_g, ln_b, w_s, bias_full, w_ba, w_bs, w_out, g_ffn)


N_EXTRACT = PEER_TOPK + 1
CAND_PAIRS = tuple((a, b) for a in range(N_EXTRACT) for b in range(N_EXTRACT) if (a + 1) * (b + 1) <= N_EXTRACT)
CAND_ROWS = -(-len(CAND_PAIRS) // 8) * 8


def _extract_top(work, n_iter, n_pad):
    del n_pad
    rows = work.shape[0]
    row = lax.broadcasted_iota(jnp.int32, work.shape, 0)
    vals = []
    for _ in range(n_iter):
        m = jnp.max(work, axis=0, keepdims=True)
        idx = jnp.min(jnp.where(work == m, row, rows), axis=0, keepdims=True)
        work = jnp.where(row == idx, -jnp.inf, work)
        vals.append(m)
    return vals, jnp.zeros_like(vals[0])


def _extract_top_distinct(work, n_iter, n_pad):
    vals = []
    for _ in range(n_iter):
        m = jnp.max(work, axis=0, keepdims=True)
        work = jnp.where(work == m, -jnp.inf, work)
        vals.append(m)
    removed = jnp.sum(jnp.where(work == -jnp.inf, 1.0, 0.0), axis=0, keepdims=True)
    return vals, jnp.abs(removed - (n_iter + n_pad))


def _route_kernel(hn_ref, wqt_ref, keys_ref, tq_ref, e0_ref, s1_ref, e1_ref, cand_ref):
    tm = hn_ref.shape[0]
    hn = hn_ref[...]
    cand_ref[...] = jnp.full(cand_ref.shape, -jnp.inf, F32)

    def stats(s0, s1, extract):
        tv0, bad0 = extract(s0, N_EXTRACT, 0)
        tv1, bad1 = extract(s1, N_EXTRACT, 0)
        for r, (a, b) in enumerate(CAND_PAIRS):
            cand_ref[r:r + 1, :] = tv0[a] + tv1[b]
        c, badc = extract(cand_ref[...], N_EXTRACT, CAND_ROWS - len(CAND_PAIRS))
        z = jnp.ones_like(c[0])
        for r in range(1, PEER_TOPK):
            z = z + jnp.exp(c[r] - c[0])
        tau = 0.5 * (c[PEER_TOPK - 1] + c[PEER_TOPK])
        return tau - s0, jnp.exp(s0 - tv0[0]) / z, tv1[0], bad0 + bad1 + badc

    def store(ref, hh, val):
        for cc in range(tm // LANES):
            ref[cc, hh] = val[:, cc * LANES:(cc + 1) * LANES]

    def head(hh, carry):
        w = wqt_ref[pl.ds(pl.multiple_of(hh * 2 * PEER_HALF, 2 * PEER_HALF), 2 * PEER_HALF), :]
        qt = _dot_nt(w, hn).astype(BF16)
        s0 = jnp.dot(keys_ref[2 * hh], qt[:PEER_HALF], preferred_element_type=F32)
        s1 = jnp.dot(keys_ref[2 * hh + 1], qt[PEER_HALF:], preferred_element_type=F32)
        tq, e0, max1, bad = stats(s0, s1, _extract_top_distinct)
        store(tq_ref, hh, tq)
        store(e0_ref, hh, e0)
        store(s1_ref, hh, s1)
        store(e1_ref, hh, jnp.exp(s1 - max1))

        @pl.when(jnp.max(bad) > 0.0)
        def _():
            tq, e0, _, _ = stats(s0, s1, _extract_top)
            store(tq_ref, hh, tq)
            store(e0_ref, hh, e0)

        return carry

    lax.fori_loop(0, PEER_HEADS, head, 0)


def _route(hn, wq_t, keys, tm=512):
    t, d = hn.shape
    nck = tm // LANES
    out_block = pl.BlockSpec((nck, PEER_HEADS, PEER_N_KEYS, LANES), lambda i: (i, 0, 0, 0))
    out_shape = jax.ShapeDtypeStruct((t // LANES, PEER_HEADS, PEER_N_KEYS, LANES), F32)
    return pl.pallas_call(
        _route_kernel,
        grid=(t // tm,),
        in_specs=[
            pl.BlockSpec((tm, d), lambda i: (i, 0)),
            _const_spec(wq_t.shape),
            _const_spec(keys.shape),
        ],
        out_specs=[out_block] * 4,
        out_shape=[out_shape] * 4,
        scratch_shapes=[pltpu.VMEM((CAND_ROWS, tm), F32)],
        compiler_params=_params(("parallel",)),
        name="peer_route",
    )(hn, wq_t, keys)


def _experts_kernel(hn_ref, down_ref, upt_ref, tq_ref, e0_ref, s1_ref, e1_ref, o_ref,
                    acc_ref, hid0_ref, hid1_ref, gh0_ref, gh1_ref, *, n_tiles):
    s = pl.program_id(0)
    te = down_ref.shape[0]
    tm = hn_ref.shape[0]

    @pl.when(s == 0)
    def _():
        for ref in (hid0_ref, hid1_ref, gh0_ref, gh1_ref):
            ref[...] = jnp.zeros_like(ref)

    @pl.when((s == 0) | (s % n_tiles == 2 % n_tiles))
    def _():
        acc_ref[...] = jnp.zeros_like(acc_ref)

    def gated_tile(hid_r, gh_w, ii, cc):
        rows = slice(ii * PEER_N_KEYS, (ii + 1) * PEER_N_KEYS)
        lanes = slice(cc * LANES, (cc + 1) * LANES)
        gate = jnp.zeros((PEER_N_KEYS, LANES), F32)
        for hh in range(PEER_HEADS):
            thr = tq_ref[cc, hh, ii:ii + 1, :]
            scale = e0_ref[cc, hh, ii:ii + 1, :]
            gate = gate + jnp.where(s1_ref[cc, hh] >= thr, e1_ref[cc, hh], 0.0) * scale
        gh_w[rows, lanes] = (gate * _gelu(hid_r[rows, lanes])).astype(BF16)

    def step(hid_w, hid_r, gh_w, gh_r):
        tiles = [(ii, cc) for ii in range(te // PEER_N_KEYS) for cc in range(tm // LANES)]
        n_pieces = 8
        per_piece = len(tiles) // n_pieces
        d = acc_ref.shape[0]
        piece = 0
        for nh in range(2):
            lanes = slice(nh * (tm // 2), (nh + 1) * (tm // 2))
            for mh in range(2):
                rows_c = slice(mh * (d // 2), (mh + 1) * (d // 2))
                acc_ref[rows_c, lanes] += jnp.dot(upt_ref[rows_c, :], gh_r[:, lanes],
                                                  preferred_element_type=F32)
                for ii, cc in tiles[piece * per_piece:(piece + 1) * per_piece]:
                    gated_tile(hid_r, gh_w, ii, cc)
                piece += 1
                rows_a = slice(mh * (te // 2), (mh + 1) * (te // 2))
                hid_w[rows_a, lanes] = _dot_nt(down_ref[rows_a, :], hn_ref[lanes, :])
                for ii, cc in tiles[piece * per_piece:(piece + 1) * per_piece]:
                    gated_tile(hid_r, gh_w, ii, cc)
                piece += 1

    @pl.when(s % 2 == 0)
    def _():
        step(hid0_ref, hid1_ref, gh1_ref, gh0_ref)

    @pl.when(s % 2 == 1)
    def _():
        step(hid1_ref, hid0_ref, gh0_ref, gh1_ref)

    @pl.when((s >= 2) & ((s - 2) % n_tiles == n_tiles - 1))
    def _():
        o_ref[...] = acc_ref[...].T


def _experts(hn, down, up_t, tq, e0, s1, e1, tm=512, te=1024):
    t, d = hn.shape
    nt, ne = t // tm, down.shape[0] // te
    nck = tm // LANES
    ni = te // PEER_N_KEYS
    assert ni % 8 == 0
    tok_a = lambda s: jnp.minimum(s // ne, nt - 1)
    tok_b = lambda s: jnp.minimum(jnp.maximum(s - 1, 0) // ne, nt - 1)
    tok_c = lambda s: jnp.minimum(jnp.maximum(s - 2, 0) // ne, nt - 1)
    exp_b = lambda s: jnp.maximum(s - 1, 0) % ne
    exp_c = lambda s: jnp.maximum(s - 2, 0) % ne
    row_block = pl.BlockSpec((nck, PEER_HEADS, ni, LANES), lambda s: (tok_b(s), 0, exp_b(s), 0))
    full_block = pl.BlockSpec((nck, PEER_HEADS, PEER_N_KEYS, LANES), lambda s: (tok_b(s), 0, 0, 0))
    return pl.pallas_call(
        functools.partial(_experts_kernel, n_tiles=ne),
        grid=(nt * ne + 2,),
        in_specs=[
            pl.BlockSpec((tm, d), lambda s: (tok_a(s), 0)),
            pl.BlockSpec((te, d), lambda s: (s % ne, 0)),
            pl.BlockSpec((d, te), lambda s: (0, exp_c(s))),
            row_block, row_block, full_block, full_block,
        ],
        out_specs=pl.BlockSpec((tm, d), lambda s: (tok_c(s), 0)),
        out_shape=jax.ShapeDtypeStruct((t, d), F32),
        scratch_shapes=[
            pltpu.VMEM((d, tm), F32),
            pltpu.VMEM((te, tm), F32), pltpu.VMEM((te, tm), F32),
            pltpu.VMEM((te, tm), BF16), pltpu.VMEM((te, tm), BF16),
        ],
        compiler_params=_params(("arbitrary",)),
        name="peer_experts",
    )(hn, down, up_t, tq, e0, s1, e1)


def _ple_kernel(h_ref, y_ref, p_ref, gple_ref, wgate_ref, wproj_ref, gfin_ref, o_ref, *, final):
    h = h_ref[...] + y_ref[...]
    gate = jax.nn.sigmoid(jnp.dot(_rms(h, gple_ref[...]).astype(BF16), wgate_ref[...], preferred_element_type=F32))
    emb = jnp.dot(p_ref[...].astype(BF16), wproj_ref[...], preferred_element_type=F32)
    h = h + gate * emb
    o_ref[...] = _rms(h, gfin_ref[...]) if final else h


def _ple(h, y, p, g_ple, w_gate, w_proj, g_final, final, tm=512):
    t, d = h.shape
    pd = p.shape[1]
    return pl.pallas_call(
        functools.partial(_ple_kernel, final=final),
        grid=(t // tm,),
        in_specs=[
            pl.BlockSpec((tm, d), lambda i: (i, 0)),
            pl.BlockSpec((tm, d), lambda i: (i, 0)),
            pl.BlockSpec((tm, pd), lambda i: (i, 0)),
            _const_spec((1, d)),
            _const_spec(w_gate.shape),
            _const_spec(w_proj.shape),
            _const_spec((1, d)),
        ],
        out_specs=pl.BlockSpec((tm, d), lambda i: (i, 0)),
        out_shape=jax.ShapeDtypeStruct((t, d), F32),
        compiler_params=_params(("parallel",)),
        name="ple",
    )(h, y, p, g_ple, w_gate, w_proj, g_final)


def kernel(x, p, norm_mix_g, w_in, sgu_ln_g, sgu_ln_b, sgu_w, sgu_b, w_branch_attn, w_branch_sgu, w_out, norm_ffn_g, peer_w_query, peer_sub_keys, peer_down, peer_up, norm_ple_g, ple_w_proj, ple_w_gate, final_norm_g):
    b, s, d = x.shape
    t = b * s
    depth = w_in.shape[0]
    h = x.reshape(t, d)
    row = lambda v: v.reshape(1, -1)
    aw = ATT_WIDTH
    for i in range(depth):
        wi = w_in[i]
        w_perm = jnp.concatenate([wi[:, 3 * aw + 2 * SGU_WIDTH:], wi[:, :3 * aw + 2 * SGU_WIDTH]], axis=1).astype(BF16)
        proj = _inproj(h, row(norm_mix_g[i]), w_perm)
        yatt = _moba(proj.reshape(b, s, -1)).reshape(t, aw)
        bias_full = jnp.repeat(sgu_b[i].T, SGU_GROUP_DIM, axis=1)
        h, hn = _tail(proj, yatt, h, row(sgu_ln_g[i]), row(sgu_ln_b[i]), sgu_w[i], bias_full,
                      w_branch_attn[i].astype(BF16), w_branch_sgu[i].astype(BF16), w_out[i].astype(BF16),
                      row(norm_ffn_g[i]))
        keys = peer_sub_keys[i].reshape(2 * PEER_HEADS, PEER_N_KEYS, PEER_HALF).astype(BF16)
        tq, e0, s1, e1 = _route(hn, peer_w_query[i].T.astype(BF16), keys)
        y = _experts(hn, peer_down[i].astype(BF16), peer_up[i].T.astype(BF16), tq, e0, s1, e1)
        h = _ple(h, y, p[i].reshape(t, -1), row(norm_ple_g[i]), ple_w_gate[i].astype(BF16),
                 ple_w_proj[i].astype(BF16), row(final_norm_g), final=(i == depth - 1))
    if depth == 0:
        raise ValueError("depth must be at least 1")
    return h.reshape(b, s, d)
```

```python
import functools

import jax
import jax.numpy as jnp
from jax import lax
from jax.experimental import pallas as pl
from jax.experimental.pallas import tpu as pltpu

F32 = jnp.float32
BF16 = jnp.bfloat16

NORM_EPS = 1e-6
ATT_HEADS = 8
ATT_HEAD_DIM = 128
ATT_WIDTH = ATT_HEADS * ATT_HEAD_DIM
MOBA_BLOCK = 256
MOBA_TOPK = 3
MOBA_BLOCKS_PER_STEP = 4
MOBA_HEADS_PER_STEP = 2
SGU_GROUPS = 8
SGU_GROUP_DIM = 128
SGU_WIDTH = SGU_GROUPS * SGU_GROUP_DIM
SGU_CHUNK = 128
PEER_HEADS = 8
PEER_N_KEYS = 128
PEER_HALF = 128
PEER_TOPK = 16

EXPERT_M_PIECES = 4
LANES = 128
VMEM_LIMIT = 56 * 1024 * 1024
MASK_NEG = -(2.0 ** 30)
LOG2E = 1.4426950408889634

COL_GATE_A = 0
COL_GATE_B = 2048
COL_Q = 4096
COL_K = COL_Q + ATT_WIDTH
COL_V = COL_K + ATT_WIDTH
COL_U = COL_V + ATT_WIDTH
COL_VG = COL_U + SGU_WIDTH


def _params(semantics, flags=None):
    return pltpu.CompilerParams(dimension_semantics=semantics, vmem_limit_bytes=VMEM_LIMIT, flags=flags)


def _rms(x, gain):
    return x * lax.rsqrt(jnp.mean(x * x, axis=-1, keepdims=True) + NORM_EPS) * gain


def _gelu(x):
    return 0.5 * x * (1.0 + lax.erf(x * (0.5 ** 0.5)))


def _dot_nt(a, b):
    return lax.dot_general(a, b, (((1,), (1,)), ((), ())), preferred_element_type=F32)


def _inproj_kernel(x_ref, g_ref, w_ref, o_ref, xn_ref):
    @pl.when(pl.program_id(1) == 0)
    def _():
        xn_ref[...] = _rms(x_ref[...], g_ref[...]).astype(BF16)

    o_ref[...] = jnp.dot(xn_ref[...], w_ref[...], preferred_element_type=F32).astype(o_ref.dtype)


def _inproj(x, gain, w, tm=1024, tn=512):
    t, d = x.shape
    n = w.shape[1]
    return pl.pallas_call(
        _inproj_kernel,
        grid=(t // tm, n // tn),
        in_specs=[
            pl.BlockSpec((tm, d), lambda i, j: (i, 0)),
            pl.BlockSpec((1, d), lambda i, j: (0, 0)),
            pl.BlockSpec((d, tn), lambda i, j: (0, j)),
        ],
        out_specs=pl.BlockSpec((tm, tn), lambda i, j: (i, j)),
        out_shape=jax.ShapeDtypeStruct((t, n), BF16),
        scratch_shapes=[pltpu.VMEM((tm, d), BF16)],
        compiler_params=_params(("parallel", "arbitrary")),
        name="inproj",
    )(x, gain, w)


def _moba_kernel(q_ref, k_ref, v_ref, o_ref, kmean_ref, vt_ref, mask_ref):
    j = pl.program_id(2)
    blk, hd, grp = MOBA_BLOCK, ATT_HEAD_DIM, MOBA_BLOCKS_PER_STEP
    heads = range(q_ref.shape[1] // hd)
    nb = k_ref.shape[0] // blk
    nbp = kmean_ref.shape[1]
    c_exp = (hd ** -0.5) * LOG2E

    def col(h):
        return slice(h * hd, (h + 1) * hd)

    @pl.when(j == 0)
    def _():
        kmean_ref[...] = jnp.zeros_like(kmean_ref)

        def body(g, c):
            for u in range(grp):
                n = g * grp + u
                rows = pl.ds(pl.multiple_of(n * blk, blk), blk)
                for h in heads:
                    kmean_ref[h, pl.ds(n, 1), :] = (jnp.sum(k_ref[rows, col(h)].astype(F32), axis=0, keepdims=True)
                                                    * (1.0 / blk))
                    vt_ref[h, g, :, u * blk:(u + 1) * blk] = v_ref[rows, col(h)].T
            return c

        lax.fori_loop(0, nb // grp, body, 0)

    own = pl.ds(pl.multiple_of(j * blk, blk), blk)
    qs = [q_ref[:, col(h)] for h in heads]
    gates = [_dot_nt(kmean_ref[h].astype(BF16), qs[h]) for h in heads]
    t_own = [_dot_nt(k_ref[own, col(h)], qs[h]) * c_exp for h in heads]

    row = lax.broadcasted_iota(jnp.int32, gates[0].shape, 0)
    for h in heads:
        g = jnp.where(row < j, gates[h], -jnp.inf)
        maskneg = jnp.full(g.shape, MASK_NEG, F32)
        for r in range(MOBA_TOPK):
            m = jnp.max(g, axis=0, keepdims=True)
            idx = jnp.min(jnp.where(g == m, row, nbp), axis=0, keepdims=True)
            idx = jnp.where(r < j, idx, -1)
            hit = row == idx
            maskneg = jnp.where(hit, 0.0, maskneg)
            g = jnp.where(hit, -jnp.inf, g)
        mask_ref[h] = maskneg

    krow = lax.broadcasted_iota(jnp.int32, (blk, blk), 0)
    qcol = lax.broadcasted_iota(jnp.int32, (blk, blk), 1)
    state = []
    for h in heads:
        t = jnp.where(krow <= qcol, t_own[h], -jnp.inf)
        m0 = jnp.max(t, axis=0, keepdims=True)
        p = jnp.exp2(t - m0)
        l0 = jnp.sum(p, axis=0, keepdims=True)
        acc0 = jnp.dot(v_ref[own, col(h)].T, p.astype(BF16), preferred_element_type=F32)
        state.append((m0, l0, acc0))

    def attend(g, carry):
        rows = pl.ds(pl.multiple_of(g * (grp * blk), grp * blk), grp * blk)
        scores = [_dot_nt(k_ref[rows, col(h)], qs[h]) * c_exp for h in heads]
        new = []
        for h in heads:
            m_prev, l_prev, acc = carry[h]
            ts = [scores[h][u * blk:(u + 1) * blk] + mask_ref[h, pl.ds(g * grp + u, 1), :] for u in range(grp)]
            m_new = m_prev
            for t in ts:
                m_new = jnp.maximum(m_new, jnp.max(t, axis=0, keepdims=True))
            alpha = jnp.exp2(m_prev - m_new)
            l_new = alpha * l_prev
            ps = []
            for t in ts:
                p = jnp.exp2(t - m_new)
                l_new = l_new + jnp.sum(p, axis=0, keepdims=True)
                ps.append(p.astype(BF16))
            acc = alpha * acc + jnp.dot(vt_ref[h, g], jnp.concatenate(ps, axis=0), preferred_element_type=F32)
            new.append((m_new, l_new, acc))
        return tuple(new)

    final = lax.fori_loop(0, (j + grp - 1) // grp, attend, tuple(state))
    for h in heads:
        _, l_fin, acc = final[h]
        o_ref[:, col(h)] = (acc / l_fin).T.astype(o_ref.dtype)


def _moba(proj3):
    b, s, _ = proj3.shape
    blk, hd, grp, nh = MOBA_BLOCK, ATT_HEAD_DIM, MOBA_BLOCKS_PER_STEP, MOBA_HEADS_PER_STEP
    nb = s // blk
    assert s % blk == 0 and nb % grp == 0 and ATT_HEADS % nh == 0
    nbp = -(-nb // 16) * 16
    wide = nh * hd
    qb, kb, vb = COL_Q // wide, COL_K // wide, COL_V // wide
    return pl.pallas_call(
        _moba_kernel,
        grid=(b, ATT_HEADS // nh, nb),
        in_specs=[
            pl.BlockSpec((None, blk, wide), lambda bi, h, j: (bi, j, qb + h)),
            pl.BlockSpec((None, s, wide), lambda bi, h, j: (bi, 0, kb + h)),
            pl.BlockSpec((None, s, wide), lambda bi, h, j: (bi, 0, vb + h)),
        ],
        out_specs=pl.BlockSpec((None, blk, wide), lambda bi, h, j: (bi, j, h)),
        out_shape=jax.ShapeDtypeStruct((b, s, ATT_WIDTH), BF16),
        scratch_shapes=[
            pltpu.VMEM((nh, nbp, hd), F32),
            pltpu.VMEM((nh, nb // grp, hd, grp * blk), BF16),
            pltpu.VMEM((nh, nbp, blk), F32),
        ],
        compiler_params=_params(("parallel", "parallel", "arbitrary")),
        name="moba",
    )(proj3, proj3, proj3)


def _tail_kernel(ga_ref, gb_ref, u_ref, vg_ref, yatt_ref, x_ref, lng_ref, lnb_ref, ws_ref, bias_ref,
                 wba_ref, wbs_ref, wout_ref, gffn_ref, h_ref, hn_ref, ysgu_ref):
    tm = u_ref.shape[0]
    ch, gd = SGU_CHUNK, SGU_GROUP_DIM
    gu = _gelu(u_ref[...].astype(F32))
    gv = _gelu(vg_ref[...].astype(F32))
    mu = jnp.mean(gv, axis=-1, keepdims=True)
    xc = gv - mu
    vn = xc * lax.rsqrt(jnp.mean(xc * xc, axis=-1, keepdims=True) + NORM_EPS)
    vn = (vn * lng_ref[...] + lnb_ref[...]).astype(BF16)
    trow = lax.broadcasted_iota(jnp.int32, (ch, ch), 0)
    tcol = lax.broadcasted_iota(jnp.int32, (ch, ch), 1)
    for g in range(SGU_GROUPS):
        w = jnp.where(tcol <= trow, ws_ref[g], 0.0).astype(BF16)
        cols = slice(g * gd, (g + 1) * gd)
        for c in range(tm // ch):
            rows = slice(c * ch, (c + 1) * ch)
            mixed = jnp.dot(w, vn[rows, cols], preferred_element_type=F32) + bias_ref[:, cols]
            ysgu_ref[rows, cols] = (gu[rows, cols] * mixed).astype(BF16)
    a = jnp.dot(yatt_ref[...], wba_ref[...], preferred_element_type=F32)
    bsg = jnp.dot(ysgu_ref[...], wbs_ref[...], preferred_element_type=F32)
    merged = (jax.nn.sigmoid(ga_ref[...].astype(F32)) * a
              + jax.nn.sigmoid(gb_ref[...].astype(F32)) * bsg)
    h = x_ref[...] + jnp.dot(merged.astype(BF16), wout_ref[...], preferred_element_type=F32)
    h_ref[...] = h
    hn_ref[...] = _rms(h, gffn_ref[...]).astype(BF16)


def _const_spec(shape):
    return pl.BlockSpec(shape, lambda *_: (0,) * len(shape), pipeline_mode=pl.Buffered(1))


def _tail(proj, yatt, x, ln_g, ln_b, w_s, bias_full, w_ba, w_bs, w_out, g_ffn, tm=256):
    t, d = x.shape
    wide, narrow = d, SGU_WIDTH
    return pl.pallas_call(
        _tail_kernel,
        grid=(t // tm,),
        in_specs=[
            pl.BlockSpec((tm, wide), lambda i: (i, COL_GATE_A // wide)),
            pl.BlockSpec((tm, wide), lambda i: (i, COL_GATE_B // wide)),
            pl.BlockSpec((tm, narrow), lambda i: (i, COL_U // narrow)),
            pl.BlockSpec((tm, narrow), lambda i: (i, COL_VG // narrow)),
            pl.BlockSpec((tm, ATT_WIDTH), lambda i: (i, 0)),
            pl.BlockSpec((tm, d), lambda i: (i, 0)),
            _const_spec((1, narrow)),
            _const_spec((1, narrow)),
            _const_spec(w_s.shape),
            _const_spec(bias_full.shape),
            _const_spec(w_ba.shape),
            _const_spec(w_bs.shape),
            _const_spec(w_out.shape),
            _const_spec((1, d)),
        ],
        out_specs=[pl.BlockSpec((tm, d), lambda i: (i, 0)), pl.BlockSpec((tm, d), lambda i: (i, 0))],
        out_shape=[jax.ShapeDtypeStruct((t, d), F32), jax.ShapeDtypeStruct((t, d), BF16)],
        scratch_shapes=[pltpu.VMEM((tm, narrow), BF16)],
        compiler_params=_params(("parallel",)),
        name="mixer_tail",
    )(proj, proj, proj, proj, yatt, x, ln_g, ln_b, w_s, bias_full, w_ba, w_bs, w_out, g_ffn)


N_EXTRACT = PEER_TOPK + 1
CAND_PAIRS = tuple((a, b) for a in range(N_EXTRACT) for b in range(N_EXTRACT) if (a + 1) * (b + 1) <= N_EXTRACT)
CAND_ROWS = -(-len(CAND_PAIRS) // 8) * 8


def _extract_top(work, n_iter, n_pad):
    del n_pad
    rows = work.shape[0]
    row = lax.broadcasted_iota(jnp.int32, work.shape, 0)
    vals = []
    for _ in range(n_iter):
        m = jnp.max(work, axis=0, keepdims=True)
        idx = jnp.min(jnp.where(work == m, row, rows), axis=0, keepdims=True)
        work = jnp.where(row == idx, -jnp.inf, work)
        vals.append(m)
    return vals, jnp.zeros_like(vals[0])


def _extract_top_distinct(work, n_iter, n_pad):
    vals = []
    for _ in range(n_iter):
        m = jnp.max(work, axis=0, keepdims=True)
        work = jnp.where(work == m, -jnp.inf, work)
        vals.append(m)
    removed = jnp.sum(jnp.where(work == -jnp.inf, 1.0, 0.0), axis=0, keepdims=True)
    return vals, jnp.abs(removed - (n_iter + n_pad))


def _route_kernel(hn_ref, wqt_ref, keys_ref, tq_ref, e0_ref, s1_ref, e1_ref, cand_ref):
    tm = hn_ref.shape[0]
    hn = hn_ref[...]
    cand_ref[...] = jnp.full(cand_ref.shape, -jnp.inf, F32)

    def stats(s0, s1, extract):
        tv0, bad0 = extract(s0, N_EXTRACT, 0)
        tv1, bad1 = extract(s1, N_EXTRACT, 0)
        for r, (a, b) in enumerate(CAND_PAIRS):
            cand_ref[r:r + 1, :] = tv0[a] + tv1[b]
        c, badc = extract(cand_ref[...], N_EXTRACT, CAND_ROWS - len(CAND_PAIRS))
        z = jnp.ones_like(c[0])
        for r in range(1, PEER_TOPK):
            z = z + jnp.exp(c[r] - c[0])
        tau = 0.5 * (c[PEER_TOPK - 1] + c[PEER_TOPK])
        return tau - s0, jnp.exp(s0 - tv0[0]) / z, tv1[0], bad0 + bad1 + badc

    def store(ref, hh, val):
        for cc in range(tm // LANES):
            ref[cc, hh] = val[:, cc * LANES:(cc + 1) * LANES]

    def head(hh, carry):
        w = wqt_ref[pl.ds(pl.multiple_of(hh * 2 * PEER_HALF, 2 * PEER_HALF), 2 * PEER_HALF), :]
        qt = _dot_nt(w, hn).astype(BF16)
        s0 = jnp.dot(keys_ref[2 * hh], qt[:PEER_HALF], preferred_element_type=F32)
        s1 = jnp.dot(keys_ref[2 * hh + 1], qt[PEER_HALF:], preferred_element_type=F32)
        tq, e0, max1, bad = stats(s0, s1, _extract_top_distinct)
        store(tq_ref, hh, tq)
        store(e0_ref, hh, e0)
        store(s1_ref, hh, s1)
        store(e1_ref, hh, jnp.exp(s1 - max1))

        @pl.when(jnp.max(bad) > 0.0)
        def _():
            tq, e0, _, _ = stats(s0, s1, _extract_top)
            store(tq_ref, hh, tq)
            store(e0_ref, hh, e0)

        return carry

    lax.fori_loop(0, PEER_HEADS, head, 0)


def _route(hn, wq_t, keys, tm=512):
    t, d = hn.shape
    nck = tm // LANES
    out_block = pl.BlockSpec((nck, PEER_HEADS, PEER_N_KEYS, LANES), lambda i: (i, 0, 0, 0))
    out_shape = jax.ShapeDtypeStruct((t // LANES, PEER_HEADS, PEER_N_KEYS, LANES), F32)
    return pl.pallas_call(
        _route_kernel,
        grid=(t // tm,),
        in_specs=[
            pl.BlockSpec((tm, d), lambda i: (i, 0)),
            _const_spec(wq_t.shape),
            _const_spec(keys.shape),
        ],
        out_specs=[out_block] * 4,
        out_shape=[out_shape] * 4,
        scratch_shapes=[pltpu.VMEM((CAND_ROWS, tm), F32)],
        compiler_params=_params(("parallel",)),
        name="peer_route",
    )(hn, wq_t, keys)


def _experts_kernel(hn_ref, down_ref, upt_ref, tq_ref, e0_ref, s1_ref, e1_ref, o_ref,
                    acc_ref, hid0_ref, hid1_ref, gh0_ref, gh1_ref, *, n_tiles):
    s = pl.program_id(0)
    te = down_ref.shape[0]
    tm = hn_ref.shape[0]

    @pl.when(s == 0)
    def _():
        for ref in (hid0_ref, hid1_ref, gh0_ref, gh1_ref):
            ref[...] = jnp.zeros_like(ref)

    @pl.when((s == 0) | (s % n_tiles == 2 % n_tiles))
    def _():
        acc_ref[...] = jnp.zeros_like(acc_ref)

    def gated_tile(hid_r, gh_w, ii, cc):
        rows = slice(ii * PEER_N_KEYS, (ii + 1) * PEER_N_KEYS)
        lanes = slice(cc * LANES, (cc + 1) * LANES)
        gate = jnp.zeros((PEER_N_KEYS, LANES), F32)
        for hh in range(PEER_HEADS):
            thr = tq_ref[cc, hh, ii:ii + 1, :]
            scale = e0_ref[cc, hh, ii:ii + 1, :]
            gate = gate + jnp.where(s1_ref[cc, hh] >= thr, e1_ref[cc, hh], 0.0) * scale
        gh_w[rows, lanes] = (gate * _gelu(hid_r[rows, lanes])).astype(BF16)

    def step(hid_w, hid_r, gh_w, gh_r):
        tiles = [(ii, cc) for ii in range(te // PEER_N_KEYS) for cc in range(tm // LANES)]
        n_m = EXPERT_M_PIECES
        per_piece = len(tiles) // (2 * n_m)
        d = acc_ref.shape[0]
        piece = 0
        for nh in range(2):
            lanes = slice(nh * (tm // 2), (nh + 1) * (tm // 2))
            for mh in range(n_m):
                for ii, cc in tiles[piece * per_piece:(piece + 1) * per_piece]:
                    gated_tile(hid_r, gh_w, ii, cc)
                piece += 1
                rows_c = slice(mh * (d // n_m), (mh + 1) * (d // n_m))
                acc_ref[rows_c, lanes] += jnp.dot(upt_ref[rows_c, :], gh_r[:, lanes],
                                                  preferred_element_type=F32)
                rows_a = slice(mh * (te // n_m), (mh + 1) * (te // n_m))
                hid_w[rows_a, lanes] = _dot_nt(down_ref[rows_a, :], hn_ref[lanes, :])

    @pl.when(s % 2 == 0)
    def _():
        step(hid0_ref, hid1_ref, gh1_ref, gh0_ref)

    @pl.when(s % 2 == 1)
    def _():
        step(hid1_ref, hid0_ref, gh0_ref, gh1_ref)

    @pl.when((s >= 2) & ((s - 2) % n_tiles == n_tiles - 1))
    def _():
        o_ref[...] = acc_ref[...].T


def _experts(hn, down, up_t, tq, e0, s1, e1, tm=512, te=1024):
    t, d = hn.shape
    nt, ne = t // tm, down.shape[0] // te
    nck = tm // LANES
    ni = te // PEER_N_KEYS
    assert ni % 8 == 0
    tok_a = lambda s: jnp.minimum(s // ne, nt - 1)
    tok_b = lambda s: jnp.minimum(jnp.maximum(s - 1, 0) // ne, nt - 1)
    tok_c = lambda s: jnp.minimum(jnp.maximum(s - 2, 0) // ne, nt - 1)
    exp_b = lambda s: jnp.maximum(s - 1, 0) % ne
    exp_c = lambda s: jnp.maximum(s - 2, 0) % ne
    row_block = pl.BlockSpec((nck, PEER_HEADS, ni, LANES), lambda s: (tok_b(s), 0, exp_b(s), 0))
    full_block = pl.BlockSpec((nck, PEER_HEADS, PEER_N_KEYS, LANES), lambda s: (tok_b(s), 0, 0, 0))
    return pl.pallas_call(
        functools.partial(_experts_kernel, n_tiles=ne),
        grid=(nt * ne + 2,),
        in_specs=[
            pl.BlockSpec((tm, d), lambda s: (tok_a(s), 0)),
            pl.BlockSpec((te, d), lambda s: (s % ne, 0)),
            pl.BlockSpec((d, te), lambda s: (0, exp_c(s))),
            row_block, row_block, full_block, full_block,
        ],
        out_specs=pl.BlockSpec((tm, d), lambda s: (tok_c(s), 0)),
        out_shape=jax.ShapeDtypeStruct((t, d), F32),
        scratch_shapes=[
            pltpu.VMEM((d, tm), F32),
            pltpu.VMEM((te, tm), F32), pltpu.VMEM((te, tm), F32),
            pltpu.VMEM((te, tm), BF16), pltpu.VMEM((te, tm), BF16),
        ],
        compiler_params=_params(("arbitrary",)),
        name="peer_experts",
    )(hn, down, up_t, tq, e0, s1, e1)


def _ple_kernel(h_ref, y_ref, p_ref, gple_ref, wgate_ref, wproj_ref, gfin_ref, o_ref, *, final):
    h = h_ref[...] + y_ref[...]
    gate = jax.nn.sigmoid(jnp.dot(_rms(h, gple_ref[...]).astype(BF16), wgate_ref[...], preferred_element_type=F32))
    emb = jnp.dot(p_ref[...].astype(BF16), wproj_ref[...], preferred_element_type=F32)
    h = h + gate * emb
    o_ref[...] = _rms(h, gfin_ref[...]) if final else h


def _ple(h, y, p, g_ple, w_gate, w_proj, g_final, final, tm=512):
    t, d = h.shape
    pd = p.shape[1]
    return pl.pallas_call(
        functools.partial(_ple_kernel, final=final),
        grid=(t // tm,),
        in_specs=[
            pl.BlockSpec((tm, d), lambda i: (i, 0)),
            pl.BlockSpec((tm, d), lambda i: (i, 0)),
            pl.BlockSpec((tm, pd), lambda i: (i, 0)),
            _const_spec((1, d)),
            _const_spec(w_gate.shape),
            _const_spec(w_proj.shape),
            _const_spec((1, d)),
        ],
        out_specs=pl.BlockSpec((tm, d), lambda i: (i, 0)),
        out_shape=jax.ShapeDtypeStruct((t, d), F32),
        compiler_params=_params(("parallel",)),
        name="ple",
    )(h, y, p, g_ple, w_gate, w_proj, g_final)


def kernel(x, p, norm_mix_g, w_in, sgu_ln_g, sgu_ln_b, sgu_w, sgu_b, w_branch_attn, w_branch_sgu, w_out, norm_ffn_g, peer_w_query, peer_sub_keys, peer_down, peer_up, norm_ple_g, ple_w_proj, ple_w_gate, final_norm_g):
    b, s, d = x.shape
    t = b * s
    depth = w_in.shape[0]
    h = x.reshape(t, d)
    row = lambda v: v.reshape(1, -1)
    aw = ATT_WIDTH
    for i in range(depth):
        wi = w_in[i]
        w_perm = jnp.concatenate([wi[:, 3 * aw + 2 * SGU_WIDTH:], wi[:, :3 * aw + 2 * SGU_WIDTH]], axis=1).astype(BF16)
        proj = _inproj(h, row(norm_mix_g[i]), w_perm)
        yatt = _moba(proj.reshape(b, s, -1)).reshape(t, aw)
        bias_full = jnp.repeat(sgu_b[i].T, SGU_GROUP_DIM, axis=1)
        h, hn = _tail(proj, yatt, h, row(sgu_ln_g[i]), row(sgu_ln_b[i]), sgu_w[i], bias_full,
                      w_branch_attn[i].astype(BF16), w_branch_sgu[i].astype(BF16), w_out[i].astype(BF16),
                      row(norm_ffn_g[i]))
        keys = peer_sub_keys[i].reshape(2 * PEER_HEADS, PEER_N_KEYS, PEER_HALF).astype(BF16)
        tq, e0, s1, e1 = _route(hn, peer_w_query[i].T.astype(BF16), keys)
        y = _experts(hn, peer_down[i].astype(BF16), peer_up[i].T.astype(BF16), tq, e0, s1, e1)
        h = _ple(h, y, p[i].reshape(t, -1), row(norm_ple_g[i]), ple_w_gate[i].astype(BF16),
                 ple_w_proj[i].astype(BF16), row(final_norm_g), final=(i == depth - 1))
    if depth == 0:
        raise ValueError("depth must be at least 1")
    return h.reshape(b, s, d)
```

```python
import functools

import jax
import jax.numpy as jnp
from jax import lax
from jax.experimental import pallas as pl
from jax.experimental.pallas import tpu as pltpu

F32 = jnp.float32
BF16 = jnp.bfloat16

NORM_EPS = 1e-6
ATT_HEADS = 8
ATT_HEAD_DIM = 128
ATT_WIDTH = ATT_HEADS * ATT_HEAD_DIM
MOBA_BLOCK = 256
MOBA_TOPK = 3
MOBA_BLOCKS_PER_STEP = 4
MOBA_HEADS_PER_STEP = 2
SGU_GROUPS = 8
SGU_GROUP_DIM = 128
SGU_WIDTH = SGU_GROUPS * SGU_GROUP_DIM
SGU_CHUNK = 128
PEER_HEADS = 8
PEER_N_KEYS = 128
PEER_HALF = 128
PEER_TOPK = 16

GATE_TILE_PARTS = 2
EXPERT_M_PIECES = 4
LANES = 128
VMEM_LIMIT = 56 * 1024 * 1024
MASK_NEG = -(2.0 ** 30)
LOG2E = 1.4426950408889634

COL_GATE_A = 0
COL_GATE_B = 2048
COL_Q = 4096
COL_K = COL_Q + ATT_WIDTH
COL_V = COL_K + ATT_WIDTH
COL_U = COL_V + ATT_WIDTH
COL_VG = COL_U + SGU_WIDTH


def _params(semantics, flags=None):
    return pltpu.CompilerParams(dimension_semantics=semantics, vmem_limit_bytes=VMEM_LIMIT, flags=flags)


def _rms(x, gain):
    return x * lax.rsqrt(jnp.mean(x * x, axis=-1, keepdims=True) + NORM_EPS) * gain


def _gelu(x):
    return 0.5 * x * (1.0 + lax.erf(x * (0.5 ** 0.5)))


def _dot_nt(a, b):
    return lax.dot_general(a, b, (((1,), (1,)), ((), ())), preferred_element_type=F32)


def _unpack(words):
    return pltpu.bitcast(words, BF16)


def _pack_kernel(x_ref, o_ref, *, transpose):
    x = x_ref[...]
    if transpose:
        x = x.T
    o_ref[...] = pltpu.bitcast(x.astype(BF16), jnp.uint32)


def _pack_bf16(x, transpose=False, tr=512):
    r, c = x.shape
    if transpose:
        out_shape, out_block, out_map = (c // 2, r), (c // 2, tr), lambda i: (0, i)
    else:
        out_shape, out_block, out_map = (r // 2, c), (tr // 2, c), lambda i: (i, 0)
    return pl.pallas_call(
        functools.partial(_pack_kernel, transpose=transpose),
        grid=(r // tr,),
        in_specs=[pl.BlockSpec((tr, c), lambda i: (i, 0))],
        out_specs=pl.BlockSpec(out_block, out_map),
        out_shape=jax.ShapeDtypeStruct(out_shape, jnp.uint32),
        compiler_params=_params(("parallel",)),
        name="pack_bf16_t" if transpose else "pack_bf16",
    )(x)


def _inproj_kernel(x_ref, g_ref, w_ref, o_ref, xn_ref):
    @pl.when(pl.program_id(1) == 0)
    def _():
        xn_ref[...] = _rms(x_ref[...], g_ref[...]).astype(BF16)

    o_ref[...] = jnp.dot(xn_ref[...], w_ref[...], preferred_element_type=F32).astype(o_ref.dtype)


def _inproj(x, gain, w, tm=1024, tn=512):
    t, d = x.shape
    n = w.shape[1]
    return pl.pallas_call(
        _inproj_kernel,
        grid=(t // tm, n // tn),
        in_specs=[
            pl.BlockSpec((tm, d), lambda i, j: (i, 0)),
            pl.BlockSpec((1, d), lambda i, j: (0, 0)),
            pl.BlockSpec((d, tn), lambda i, j: (0, j)),
        ],
        out_specs=pl.BlockSpec((tm, tn), lambda i, j: (i, j)),
        out_shape=jax.ShapeDtypeStruct((t, n), BF16),
        scratch_shapes=[pltpu.VMEM((tm, d), BF16)],
        compiler_params=_params(("parallel", "arbitrary")),
        name="inproj",
    )(x, gain, w)


def _moba_kernel(q_ref, k_ref, v_ref, o_ref, kmean_ref, vt_ref, mask_ref):
    j = pl.program_id(2)
    blk, hd, grp = MOBA_BLOCK, ATT_HEAD_DIM, MOBA_BLOCKS_PER_STEP
    heads = range(q_ref.shape[1] // hd)
    nb = k_ref.shape[0] // blk
    nbp = kmean_ref.shape[1]
    c_exp = (hd ** -0.5) * LOG2E

    def col(h):
        return slice(h * hd, (h + 1) * hd)

    @pl.when(j == 0)
    def _():
        kmean_ref[...] = jnp.zeros_like(kmean_ref)

        def body(g, c):
            for u in range(grp):
                n = g * grp + u
                rows = pl.ds(pl.multiple_of(n * blk, blk), blk)
                for h in heads:
                    kmean_ref[h, pl.ds(n, 1), :] = (jnp.sum(k_ref[rows, col(h)].astype(F32), axis=0, keepdims=True)
                                                    * (1.0 / blk))
                    vt_ref[h, g, :, u * blk:(u + 1) * blk] = v_ref[rows, col(h)].T
            return c

        lax.fori_loop(0, nb // grp, body, 0)

    own = pl.ds(pl.multiple_of(j * blk, blk), blk)
    qs = [q_ref[:, col(h)] for h in heads]
    gates = [_dot_nt(kmean_ref[h].astype(BF16), qs[h]) for h in heads]
    t_own = [_dot_nt(k_ref[own, col(h)], qs[h]) * c_exp for h in heads]

    row = lax.broadcasted_iota(jnp.int32, gates[0].shape, 0)
    for h in heads:
        g = jnp.where(row < j, gates[h], -jnp.inf)
        maskneg = jnp.full(g.shape, MASK_NEG, F32)
        for r in range(MOBA_TOPK):
            m = jnp.max(g, axis=0, keepdims=True)
            idx = jnp.min(jnp.where(g == m, row, nbp), axis=0, keepdims=True)
            idx = jnp.where(r < j, idx, -1)
            hit = row == idx
            maskneg = jnp.where(hit, 0.0, maskneg)
            g = jnp.where(hit, -jnp.inf, g)
        mask_ref[h] = maskneg

    krow = lax.broadcasted_iota(jnp.int32, (blk, blk), 0)
    qcol = lax.broadcasted_iota(jnp.int32, (blk, blk), 1)
    state = []
    for h in heads:
        t = jnp.where(krow <= qcol, t_own[h], -jnp.inf)
        m0 = jnp.max(t, axis=0, keepdims=True)
        p = jnp.exp2(t - m0)
        l0 = jnp.sum(p, axis=0, keepdims=True)
        acc0 = jnp.dot(v_ref[own, col(h)].T, p.astype(BF16), preferred_element_type=F32)
        state.append((m0, l0, acc0))

    def attend(g, carry):
        rows = pl.ds(pl.multiple_of(g * (grp * blk), grp * blk), grp * blk)
        scores = [_dot_nt(k_ref[rows, col(h)], qs[h]) * c_exp for h in heads]
        new = []
        for h in heads:
            m_prev, l_prev, acc = carry[h]
            ts = [scores[h][u * blk:(u + 1) * blk] + mask_ref[h, pl.ds(g * grp + u, 1), :] for u in range(grp)]
            m_new = m_prev
            for t in ts:
                m_new = jnp.maximum(m_new, jnp.max(t, axis=0, keepdims=True))
            alpha = jnp.exp2(m_prev - m_new)
            l_new = alpha * l_prev
            ps = []
            for t in ts:
                p = jnp.exp2(t - m_new)
                l_new = l_new + jnp.sum(p, axis=0, keepdims=True)
                ps.append(p.astype(BF16))
            acc = alpha * acc + jnp.dot(vt_ref[h, g], jnp.concatenate(ps, axis=0), preferred_element_type=F32)
            new.append((m_new, l_new, acc))
        return tuple(new)

    final = lax.fori_loop(0, (j + grp - 1) // grp, attend, tuple(state))
    for h in heads:
        _, l_fin, acc = final[h]
        o_ref[:, col(h)] = (acc / l_fin).T.astype(o_ref.dtype)


def _moba(proj3):
    b, s, _ = proj3.shape
    blk, hd, grp, nh = MOBA_BLOCK, ATT_HEAD_DIM, MOBA_BLOCKS_PER_STEP, MOBA_HEADS_PER_STEP
    nb = s // blk
    assert s % blk == 0 and nb % grp == 0 and ATT_HEADS % nh == 0
    nbp = -(-nb // 16) * 16
    wide = nh * hd
    qb, kb, vb = COL_Q // wide, COL_K // wide, COL_V // wide
    return pl.pallas_call(
        _moba_kernel,
        grid=(b, ATT_HEADS // nh, nb),
        in_specs=[
            pl.BlockSpec((None, blk, wide), lambda bi, h, j: (bi, j, qb + h)),
            pl.BlockSpec((None, s, wide), lambda bi, h, j: (bi, 0, kb + h)),
            pl.BlockSpec((None, s, wide), lambda bi, h, j: (bi, 0, vb + h)),
        ],
        out_specs=pl.BlockSpec((None, blk, wide), lambda bi, h, j: (bi, j, h)),
        out_shape=jax.ShapeDtypeStruct((b, s, ATT_WIDTH), BF16),
        scratch_shapes=[
            pltpu.VMEM((nh, nbp, hd), F32),
            pltpu.VMEM((nh, nb // grp, hd, grp * blk), BF16),
            pltpu.VMEM((nh, nbp, blk), F32),
        ],
        compiler_params=_params(("parallel", "parallel", "arbitrary")),
        name="moba",
    )(proj3, proj3, proj3)


def _tail_kernel(ga_ref, gb_ref, u_ref, vg_ref, yatt_ref, x_ref, lng_ref, lnb_ref, ws_ref, bias_ref,
                 wba_ref, wbs_ref, wout_ref, gffn_ref, h_ref, hn_ref, ysgu_ref):
    tm = u_ref.shape[0]
    ch, gd = SGU_CHUNK, SGU_GROUP_DIM
    gu = _gelu(u_ref[...].astype(F32))
    gv = _gelu(vg_ref[...].astype(F32))
    mu = jnp.mean(gv, axis=-1, keepdims=True)
    xc = gv - mu
    vn = xc * lax.rsqrt(jnp.mean(xc * xc, axis=-1, keepdims=True) + NORM_EPS)
    vn = (vn * lng_ref[...] + lnb_ref[...]).astype(BF16)
    trow = lax.broadcasted_iota(jnp.int32, (ch, ch), 0)
    tcol = lax.broadcasted_iota(jnp.int32, (ch, ch), 1)
    for g in range(SGU_GROUPS):
        w = jnp.where(tcol <= trow, ws_ref[g], 0.0).astype(BF16)
        cols = slice(g * gd, (g + 1) * gd)
        for c in range(tm // ch):
            rows = slice(c * ch, (c + 1) * ch)
            mixed = jnp.dot(w, vn[rows, cols], preferred_element_type=F32) + bias_ref[:, cols]
            ysgu_ref[rows, cols] = (gu[rows, cols] * mixed).astype(BF16)
    a = jnp.dot(yatt_ref[...], wba_ref[...], preferred_element_type=F32)
    bsg = jnp.dot(ysgu_ref[...], wbs_ref[...], preferred_element_type=F32)
    merged = (jax.nn.sigmoid(ga_ref[...].astype(F32)) * a
              + jax.nn.sigmoid(gb_ref[...].astype(F32)) * bsg)
    h = x_ref[...] + jnp.dot(merged.astype(BF16), wout_ref[...], preferred_element_type=F32)
    h_ref[...] = h
    hn_ref[...] = pltpu.bitcast(_rms(h, gffn_ref[...]).astype(BF16), jnp.uint32)


def _const_spec(shape):
    return pl.BlockSpec(shape, lambda *_: (0,) * len(shape), pipeline_mode=pl.Buffered(1))


def _tail(proj, yatt, x, ln_g, ln_b, w_s, bias_full, w_ba, w_bs, w_out, g_ffn, tm=256):
    t, d = x.shape
    wide, narrow = d, SGU_WIDTH
    return pl.pallas_call(
        _tail_kernel,
        grid=(t // tm,),
        in_specs=[
            pl.BlockSpec((tm, wide), lambda i: (i, COL_GATE_A // wide)),
            pl.BlockSpec((tm, wide), lambda i: (i, COL_GATE_B // wide)),
            pl.BlockSpec((tm, narrow), lambda i: (i, COL_U // narrow)),
            pl.BlockSpec((tm, narrow), lambda i: (i, COL_VG // narrow)),
            pl.BlockSpec((tm, ATT_WIDTH), lambda i: (i, 0)),
            pl.BlockSpec((tm, d), lambda i: (i, 0)),
            _const_spec((1, narrow)),
            _const_spec((1, narrow)),
            _const_spec(w_s.shape),
            _const_spec(bias_full.shape),
            _const_spec(w_ba.shape),
            _const_spec(w_bs.shape),
            _const_spec(w_out.shape),
            _const_spec((1, d)),
        ],
        out_specs=[pl.BlockSpec((tm, d), lambda i: (i, 0)), pl.BlockSpec((tm // 2, d), lambda i: (i, 0))],
        out_shape=[jax.ShapeDtypeStruct((t, d), F32), jax.ShapeDtypeStruct((t // 2, d), jnp.uint32)],
        scratch_shapes=[pltpu.VMEM((tm, narrow), BF16)],
        compiler_params=_params(("parallel",)),
        name="mixer_tail",
    )(proj, proj, proj, proj, yatt, x, ln_g, ln_b, w_s, bias_full, w_ba, w_bs, w_out, g_ffn)


N_EXTRACT = PEER_TOPK + 1
CAND_PAIRS = tuple((a, b) for a in range(N_EXTRACT) for b in range(N_EXTRACT) if (a + 1) * (b + 1) <= N_EXTRACT)
CAND_ROWS = -(-len(CAND_PAIRS) // 8) * 8


def _extract_top(work, n_iter, n_pad):
    del n_pad
    rows = work.shape[0]
    row = lax.broadcasted_iota(jnp.int32, work.shape, 0)
    vals = []
    for _ in range(n_iter):
        m = jnp.max(work, axis=0, keepdims=True)
        idx = jnp.min(jnp.where(work == m, row, rows), axis=0, keepdims=True)
        work = jnp.where(row == idx, -jnp.inf, work)
        vals.append(m)
    return vals, jnp.zeros_like(vals[0])


def _extract_top_distinct(work, n_iter, n_pad):
    vals = []
    for _ in range(n_iter):
        m = jnp.max(work, axis=0, keepdims=True)
        work = jnp.where(work == m, -jnp.inf, work)
        vals.append(m)
    removed = jnp.sum(jnp.where(work == -jnp.inf, 1.0, 0.0), axis=0, keepdims=True)
    return vals, jnp.abs(removed - (n_iter + n_pad))


def _route_kernel(hn_ref, wqt_ref, keys_ref, tq_ref, e0_ref, s1_ref, e1_ref, cand_ref):
    hn = _unpack(hn_ref[...])
    tm = hn.shape[0]
    cand_ref[...] = jnp.full(cand_ref.shape, -jnp.inf, F32)

    def stats(s0, s1, extract):
        tv0, bad0 = extract(s0, N_EXTRACT, 0)
        tv1, bad1 = extract(s1, N_EXTRACT, 0)
        for r, (a, b) in enumerate(CAND_PAIRS):
            cand_ref[r:r + 1, :] = tv0[a] + tv1[b]
        c, badc = extract(cand_ref[...], N_EXTRACT, CAND_ROWS - len(CAND_PAIRS))
        z = jnp.ones_like(c[0])
        for r in range(1, PEER_TOPK):
            z = z + jnp.exp(c[r] - c[0])
        tau = 0.5 * (c[PEER_TOPK - 1] + c[PEER_TOPK])
        return tau - s0, jnp.exp(s0 - tv0[0]) / z, tv1[0], bad0 + bad1 + badc

    def store(ref, hh, val):
        for cc in range(tm // LANES):
            ref[cc, hh] = val[:, cc * LANES:(cc + 1) * LANES]

    def head(hh, carry):
        w = wqt_ref[pl.ds(pl.multiple_of(hh * 2 * PEER_HALF, 2 * PEER_HALF), 2 * PEER_HALF), :]
        qt = _dot_nt(w, hn).astype(BF16)
        s0 = jnp.dot(keys_ref[2 * hh], qt[:PEER_HALF], preferred_element_type=F32)
        s1 = jnp.dot(keys_ref[2 * hh + 1], qt[PEER_HALF:], preferred_element_type=F32)
        tq, e0, max1, bad = stats(s0, s1, _extract_top_distinct)
        store(tq_ref, hh, tq)
        store(e0_ref, hh, e0)
        store(s1_ref, hh, s1)
        store(e1_ref, hh, jnp.exp(s1 - max1))

        @pl.when(jnp.max(bad) > 0.0)
        def _():
            tq, e0, _, _ = stats(s0, s1, _extract_top)
            store(tq_ref, hh, tq)
            store(e0_ref, hh, e0)

        return carry

    lax.fori_loop(0, PEER_HEADS, head, 0)


def _route(hn, wq_t, keys, tm=512):
    t, d = 2 * hn.shape[0], hn.shape[1]
    nck = tm // LANES
    out_block = pl.BlockSpec((nck, PEER_HEADS, PEER_N_KEYS, LANES), lambda i: (i, 0, 0, 0))
    out_shape = jax.ShapeDtypeStruct((t // LANES, PEER_HEADS, PEER_N_KEYS, LANES), F32)
    return pl.pallas_call(
        _route_kernel,
        grid=(t // tm,),
        in_specs=[
            pl.BlockSpec((tm // 2, d), lambda i: (i, 0)),
            _const_spec(wq_t.shape),
            _const_spec(keys.shape),
        ],
        out_specs=[out_block] * 4,
        out_shape=[out_shape] * 4,
        scratch_shapes=[pltpu.VMEM((CAND_ROWS, tm), F32)],
        compiler_params=_params(("parallel",)),
        name="peer_route",
    )(hn, wq_t, keys)


def _experts_kernel(hn_ref, down_ref, upt_ref, tq_ref, e0_ref, s1_ref, e1_ref, o_ref,
                    acc_ref, hid0_ref, hid1_ref, gh0_ref, gh1_ref, *, n_tiles):
    s = pl.program_id(0)
    te = 2 * down_ref.shape[0]
    tm = 2 * hn_ref.shape[0]

    @pl.when(s == 0)
    def _():
        for ref in (hid0_ref, hid1_ref, gh0_ref, gh1_ref):
            ref[...] = jnp.zeros_like(ref)

    @pl.when((s == 0) | (s % n_tiles == 2 % n_tiles))
    def _():
        acc_ref[...] = jnp.zeros_like(acc_ref)

    def gated_tile(hid_r, gh_w, ii, cc, part):
        n_sub = PEER_N_KEYS // GATE_TILE_PARTS
        sub = slice(part * n_sub, (part + 1) * n_sub)
        rows = slice(ii * PEER_N_KEYS + part * n_sub, ii * PEER_N_KEYS + (part + 1) * n_sub)
        lanes = slice(cc * LANES, (cc + 1) * LANES)
        gate = jnp.zeros((n_sub, LANES), F32)
        for hh in range(PEER_HEADS):
            thr = tq_ref[cc, hh, ii:ii + 1, :]
            scale = e0_ref[cc, hh, ii:ii + 1, :]
            gate = gate + jnp.where(s1_ref[cc, hh, sub, :] >= thr, e1_ref[cc, hh, sub, :], 0.0) * scale
        gh_w[rows, lanes] = (gate * _gelu(hid_r[rows, lanes])).astype(BF16)

    def step(hid_w, hid_r, gh_w, gh_r):
        tiles = [(ii, cc) for ii in range(te // PEER_N_KEYS) for cc in range(tm // LANES)]
        n_m = EXPERT_M_PIECES
        n_k = len(tiles) // (2 * n_m)
        d = acc_ref.shape[0]
        tile = iter(tiles)
        for nh in range(2):
            lanes = slice(nh * (tm // 2), (nh + 1) * (tm // 2))
            words_t = slice(nh * (tm // 4), (nh + 1) * (tm // 4))
            for mh in range(n_m):
                rows_c = slice(mh * (d // n_m), (mh + 1) * (d // n_m))
                words_c = slice(mh * (d // n_m // 2), (mh + 1) * (d // n_m // 2))
                rows_a = slice(mh * (te // n_m), (mh + 1) * (te // n_m))
                words_a = slice(mh * (te // n_m // 2), (mh + 1) * (te // n_m // 2))
                up = down = None
                for kc in range(n_k):
                    ii, cc = next(tile)
                    gated_tile(hid_r, gh_w, ii, cc, 0)
                    kk_a = slice(kc * (d // n_k), (kc + 1) * (d // n_k))
                    part = _dot_nt(_unpack(down_ref[words_a, kk_a]), _unpack(hn_ref[words_t, kk_a]))
                    down = part if down is None else down + part
                    gated_tile(hid_r, gh_w, ii, cc, 1)
                    kk_c = slice(kc * (te // n_k), (kc + 1) * (te // n_k))
                    part = jnp.dot(_unpack(upt_ref[words_c, kk_c]), gh_r[kk_c, lanes],
                                   preferred_element_type=F32)
                    up = part if up is None else up + part
                hid_w[rows_a, lanes] = down
                acc_ref[rows_c, lanes] += up

    @pl.when(s % 2 == 0)
    def _():
        step(hid0_ref, hid1_ref, gh1_ref, gh0_ref)

    @pl.when(s % 2 == 1)
    def _():
        step(hid1_ref, hid0_ref, gh0_ref, gh1_ref)

    @pl.when((s >= 2) & ((s - 2) % n_tiles == n_tiles - 1))
    def _():
        o_ref[...] = acc_ref[...].T


def _experts(hn, down, up_t, tq, e0, s1, e1, tm=512, te=1024):
    t, d = 2 * hn.shape[0], hn.shape[1]
    nt, ne = t // tm, 2 * down.shape[0] // te
    nck = tm // LANES
    ni = te // PEER_N_KEYS
    assert ni % 8 == 0
    tok_a = lambda s: jnp.minimum(s // ne, nt - 1)
    tok_b = lambda s: jnp.minimum(jnp.maximum(s - 1, 0) // ne, nt - 1)
    tok_c = lambda s: jnp.minimum(jnp.maximum(s - 2, 0) // ne, nt - 1)
    exp_b = lambda s: jnp.maximum(s - 1, 0) % ne
    exp_c = lambda s: jnp.maximum(s - 2, 0) % ne
    row_block = pl.BlockSpec((nck, PEER_HEADS, ni, LANES), lambda s: (tok_b(s), 0, exp_b(s), 0))
    full_block = pl.BlockSpec((nck, PEER_HEADS, PEER_N_KEYS, LANES), lambda s: (tok_b(s), 0, 0, 0))
    return pl.pallas_call(
        functools.partial(_experts_kernel, n_tiles=ne),
        grid=(nt * ne + 2,),
        in_specs=[
            pl.BlockSpec((tm // 2, d), lambda s: (tok_a(s), 0)),
            pl.BlockSpec((te // 2, d), lambda s: (s % ne, 0)),
            pl.BlockSpec((d // 2, te), lambda s: (0, exp_c(s))),
            row_block, row_block, full_block, full_block,
        ],
        out_specs=pl.BlockSpec((tm, d), lambda s: (tok_c(s), 0)),
        out_shape=jax.ShapeDtypeStruct((t, d), F32),
        scratch_shapes=[
            pltpu.VMEM((d, tm), F32),
            pltpu.VMEM((te, tm), F32), pltpu.VMEM((te, tm), F32),
            pltpu.VMEM((te, tm), BF16), pltpu.VMEM((te, tm), BF16),
        ],
        compiler_params=_params(("arbitrary",)),
        name="peer_experts",
    )(hn, down, up_t, tq, e0, s1, e1)


def _ple_kernel(h_ref, y_ref, p_ref, gple_ref, wgate_ref, wproj_ref, gfin_ref, o_ref, *, final):
    h = h_ref[...] + y_ref[...]
    gate = jax.nn.sigmoid(jnp.dot(_rms(h, gple_ref[...]).astype(BF16), wgate_ref[...], preferred_element_type=F32))
    emb = jnp.dot(p_ref[...].astype(BF16), wproj_ref[...], preferred_element_type=F32)
    h = h + gate * emb
    o_ref[...] = _rms(h, gfin_ref[...]) if final else h


def _ple(h, y, p, g_ple, w_gate, w_proj, g_final, final, tm=512):
    t, d = h.shape
    pd = p.shape[1]
    return pl.pallas_call(
        functools.partial(_ple_kernel, final=final),
        grid=(t // tm,),
        in_specs=[
            pl.BlockSpec((tm, d), lambda i: (i, 0)),
            pl.BlockSpec((tm, d), lambda i: (i, 0)),
            pl.BlockSpec((tm, pd), lambda i: (i, 0)),
            _const_spec((1, d)),
            _const_spec(w_gate.shape),
            _const_spec(w_proj.shape),
            _const_spec((1, d)),
        ],
        out_specs=pl.BlockSpec((tm, d), lambda i: (i, 0)),
        out_shape=jax.ShapeDtypeStruct((t, d), F32),
        compiler_params=_params(("parallel",)),
        name="ple",
    )(h, y, p, g_ple, w_gate, w_proj, g_final)


def kernel(x, p, norm_mix_g, w_in, sgu_ln_g, sgu_ln_b, sgu_w, sgu_b, w_branch_attn, w_branch_sgu, w_out, norm_ffn_g, peer_w_query, peer_sub_keys, peer_down, peer_up, norm_ple_g, ple_w_proj, ple_w_gate, final_norm_g):
    b, s, d = x.shape
    t = b * s
    depth = w_in.shape[0]
    h = x.reshape(t, d)
    row = lambda v: v.reshape(1, -1)
    aw = ATT_WIDTH
    for i in range(depth):
        wi = w_in[i]
        w_perm = jnp.concatenate([wi[:, 3 * aw + 2 * SGU_WIDTH:], wi[:, :3 * aw + 2 * SGU_WIDTH]], axis=1).astype(BF16)
        proj = _inproj(h, row(norm_mix_g[i]), w_perm)
        yatt = _moba(proj.reshape(b, s, -1)).reshape(t, aw)
        bias_full = jnp.repeat(sgu_b[i].T, SGU_GROUP_DIM, axis=1)
        h, hn = _tail(proj, yatt, h, row(sgu_ln_g[i]), row(sgu_ln_b[i]), sgu_w[i], bias_full,
                      w_branch_attn[i].astype(BF16), w_branch_sgu[i].astype(BF16), w_out[i].astype(BF16),
                      row(norm_ffn_g[i]))
        keys = peer_sub_keys[i].reshape(2 * PEER_HEADS, PEER_N_KEYS, PEER_HALF).astype(BF16)
        tq, e0, s1, e1 = _route(hn, peer_w_query[i].T.astype(BF16), keys)
        y = _experts(hn, _pack_bf16(peer_down[i]), _pack_bf16(peer_up[i], transpose=True), tq, e0, s1, e1)
        h = _ple(h, y, p[i].reshape(t, -1), row(norm_ple_g[i]), ple_w_gate[i].astype(BF16),
                 ple_w_proj[i].astype(BF16), row(final_norm_g), final=(i == depth - 1))
    if depth == 0:
        raise ValueError("depth must be at least 1")
    return h.reshape(b, s, d)
```

```python
import functools

import jax
import jax.numpy as jnp
from jax import lax
from jax.experimental import pallas as pl
from jax.experimental.pallas import tpu as pltpu

F32 = jnp.float32
BF16 = jnp.bfloat16

NORM_EPS = 1e-6
ATT_HEADS = 8
ATT_HEAD_DIM = 128
ATT_WIDTH = ATT_HEADS * ATT_HEAD_DIM
MOBA_BLOCK = 256
MOBA_TOPK = 3
MOBA_BLOCKS_PER_STEP = 4
MOBA_HEADS_PER_STEP = 4
SGU_GROUPS = 8
SGU_GROUP_DIM = 128
SGU_WIDTH = SGU_GROUPS * SGU_GROUP_DIM
SGU_CHUNK = 128
PEER_HEADS = 8
PEER_N_KEYS = 128
PEER_HALF = 128
PEER_TOPK = 16

GATE_TILE_PARTS = 2
EXPERT_M_PIECES = 4
LANES = 128
VMEM_LIMIT = 56 * 1024 * 1024
MASK_NEG = -(2.0 ** 30)
LOG2E = 1.4426950408889634

COL_GATE_A = 0
COL_GATE_B = 2048
COL_Q = 4096
COL_K = COL_Q + ATT_WIDTH
COL_V = COL_K + ATT_WIDTH
COL_U = COL_V + ATT_WIDTH
COL_VG = COL_U + SGU_WIDTH


def _params(semantics, flags=None):
    return pltpu.CompilerParams(dimension_semantics=semantics, vmem_limit_bytes=VMEM_LIMIT, flags=flags)


def _rms(x, gain):
    return x * lax.rsqrt(jnp.mean(x * x, axis=-1, keepdims=True) + NORM_EPS) * gain


def _gelu(x):
    return 0.5 * x * (1.0 + lax.erf(x * (0.5 ** 0.5)))


def _dot_nt(a, b):
    return lax.dot_general(a, b, (((1,), (1,)), ((), ())), preferred_element_type=F32)


def _unpack(words):
    return pltpu.bitcast(words, BF16)


def _pack_kernel(x_ref, o_ref, *, transpose):
    x = x_ref[...]
    if transpose:
        x = x.T
    o_ref[...] = pltpu.bitcast(x.astype(BF16), jnp.uint32)


def _pack_bf16(x, transpose=False, tr=512):
    r, c = x.shape
    if transpose:
        out_shape, out_block, out_map = (c // 2, r), (c // 2, tr), lambda i: (0, i)
    else:
        out_shape, out_block, out_map = (r // 2, c), (tr // 2, c), lambda i: (i, 0)
    return pl.pallas_call(
        functools.partial(_pack_kernel, transpose=transpose),
        grid=(r // tr,),
        in_specs=[pl.BlockSpec((tr, c), lambda i: (i, 0))],
        out_specs=pl.BlockSpec(out_block, out_map),
        out_shape=jax.ShapeDtypeStruct(out_shape, jnp.uint32),
        compiler_params=_params(("parallel",)),
        name="pack_bf16_t" if transpose else "pack_bf16",
    )(x)


def _inproj_kernel(x_ref, g_ref, w_ref, o_ref, xn_ref):
    @pl.when(pl.program_id(1) == 0)
    def _():
        xn_ref[...] = _rms(x_ref[...], g_ref[...]).astype(BF16)

    o_ref[...] = jnp.dot(xn_ref[...], w_ref[...], preferred_element_type=F32).astype(o_ref.dtype)


def _inproj(x, gain, w, tm=1024, tn=512):
    t, d = x.shape
    n = w.shape[1]
    return pl.pallas_call(
        _inproj_kernel,
        grid=(t // tm, n // tn),
        in_specs=[
            pl.BlockSpec((tm, d), lambda i, j: (i, 0)),
            pl.BlockSpec((1, d), lambda i, j: (0, 0)),
            pl.BlockSpec((d, tn), lambda i, j: (0, j)),
        ],
        out_specs=pl.BlockSpec((tm, tn), lambda i, j: (i, j)),
        out_shape=jax.ShapeDtypeStruct((t, n), BF16),
        scratch_shapes=[pltpu.VMEM((tm, d), BF16)],
        compiler_params=_params(("parallel", "arbitrary")),
        name="inproj",
    )(x, gain, w)


def _moba_kernel(q_ref, k_ref, v_ref, o_ref, kmean_ref, vt_ref, mask_ref):
    j = pl.program_id(2)
    blk, hd, grp = MOBA_BLOCK, ATT_HEAD_DIM, MOBA_BLOCKS_PER_STEP
    heads = range(q_ref.shape[1] // hd)
    nb = k_ref.shape[0] // blk
    nbp = kmean_ref.shape[1]
    c_exp = (hd ** -0.5) * LOG2E

    def col(h):
        return slice(h * hd, (h + 1) * hd)

    @pl.when(j == 0)
    def _():
        kmean_ref[...] = jnp.zeros_like(kmean_ref)

        def body(g, c):
            for u in range(grp):
                n = g * grp + u
                rows = pl.ds(pl.multiple_of(n * blk, blk), blk)
                for h in heads:
                    kmean_ref[h, pl.ds(n, 1), :] = (jnp.sum(k_ref[rows, col(h)].astype(F32), axis=0, keepdims=True)
                                                    * (1.0 / blk))
                    vt_ref[h, g, :, u * blk:(u + 1) * blk] = v_ref[rows, col(h)].T
            return c

        lax.fori_loop(0, nb // grp, body, 0)

    own = pl.ds(pl.multiple_of(j * blk, blk), blk)
    qs = [q_ref[:, col(h)] for h in heads]
    gates = [_dot_nt(kmean_ref[h].astype(BF16), qs[h]) for h in heads]
    t_own = [_dot_nt(k_ref[own, col(h)], qs[h]) * c_exp for h in heads]

    row = lax.broadcasted_iota(jnp.int32, gates[0].shape, 0)
    for h in heads:
        g = jnp.where(row < j, gates[h], -jnp.inf)
        maskneg = jnp.full(g.shape, MASK_NEG, F32)
        for r in range(MOBA_TOPK):
            m = jnp.max(g, axis=0, keepdims=True)
            idx = jnp.min(jnp.where(g == m, row, nbp), axis=0, keepdims=True)
            idx = jnp.where(r < j, idx, -1)
            hit = row == idx
            maskneg = jnp.where(hit, 0.0, maskneg)
            g = jnp.where(hit, -jnp.inf, g)
        mask_ref[h] = maskneg

    krow = lax.broadcasted_iota(jnp.int32, (blk, blk), 0)
    qcol = lax.broadcasted_iota(jnp.int32, (blk, blk), 1)
    state = []
    for h in heads:
        t = jnp.where(krow <= qcol, t_own[h], -jnp.inf)
        m0 = jnp.max(t, axis=0, keepdims=True)
        p = jnp.exp2(t - m0)
        l0 = jnp.sum(p, axis=0, keepdims=True)
        acc0 = jnp.dot(v_ref[own, col(h)].T, p.astype(BF16), preferred_element_type=F32)
        state.append((m0, l0, acc0))

    def attend(g, carry):
        rows = pl.ds(pl.multiple_of(g * (grp * blk), grp * blk), grp * blk)
        scores = [_dot_nt(k_ref[rows, col(h)], qs[h]) * c_exp for h in heads]
        new = []
        for h in heads:
            m_prev, l_prev, acc = carry[h]
            ts = [scores[h][u * blk:(u + 1) * blk] + mask_ref[h, pl.ds(g * grp + u, 1), :] for u in range(grp)]
            m_new = m_prev
            for t in ts:
                m_new = jnp.maximum(m_new, jnp.max(t, axis=0, keepdims=True))
            alpha = jnp.exp2(m_prev - m_new)
            l_new = alpha * l_prev
            ps = []
            for t in ts:
                p = jnp.exp2(t - m_new)
                l_new = l_new + jnp.sum(p, axis=0, keepdims=True)
                ps.append(p.astype(BF16))
            acc = alpha * acc + jnp.dot(vt_ref[h, g], jnp.concatenate(ps, axis=0), preferred_element_type=F32)
            new.append((m_new, l_new, acc))
        return tuple(new)

    final = lax.fori_loop(0, (j + grp - 1) // grp, attend, tuple(state))
    for h in heads:
        _, l_fin, acc = final[h]
        o_ref[:, col(h)] = (acc / l_fin).T.astype(o_ref.dtype)


def _moba(proj3):
    b, s, _ = proj3.shape
    blk, hd, grp, nh = MOBA_BLOCK, ATT_HEAD_DIM, MOBA_BLOCKS_PER_STEP, MOBA_HEADS_PER_STEP
    nb = s // blk
    assert s % blk == 0 and nb % grp == 0 and ATT_HEADS % nh == 0
    nbp = -(-nb // 16) * 16
    wide = nh * hd
    qb, kb, vb = COL_Q // wide, COL_K // wide, COL_V // wide
    return pl.pallas_call(
        _moba_kernel,
        grid=(b, ATT_HEADS // nh, nb),
        in_specs=[
            pl.BlockSpec((None, blk, wide), lambda bi, h, j: (bi, j, qb + h)),
            pl.BlockSpec((None, s, wide), lambda bi, h, j: (bi, 0, kb + h)),
            pl.BlockSpec((None, s, wide), lambda bi, h, j: (bi, 0, vb + h)),
        ],
        out_specs=pl.BlockSpec((None, blk, wide), lambda bi, h, j: (bi, j, h)),
        out_shape=jax.ShapeDtypeStruct((b, s, ATT_WIDTH), BF16),
        scratch_shapes=[
            pltpu.VMEM((nh, nbp, hd), F32),
            pltpu.VMEM((nh, nb // grp, hd, grp * blk), BF16),
            pltpu.VMEM((nh, nbp, blk), F32),
        ],
        compiler_params=_params(("parallel", "parallel", "arbitrary")),
        name="moba",
    )(proj3, proj3, proj3)


def _tail_kernel(ga_ref, gb_ref, u_ref, vg_ref, yatt_ref, x_ref, lng_ref, lnb_ref, ws_ref, bias_ref,
                 wba_ref, wbs_ref, wout_ref, gffn_ref, h_ref, hn_ref, ysgu_ref):
    tm = u_ref.shape[0]
    ch, gd = SGU_CHUNK, SGU_GROUP_DIM
    gu = _gelu(u_ref[...].astype(F32))
    gv = _gelu(vg_ref[...].astype(F32))
    mu = jnp.mean(gv, axis=-1, keepdims=True)
    xc = gv - mu
    vn = xc * lax.rsqrt(jnp.mean(xc * xc, axis=-1, keepdims=True) + NORM_EPS)
    vn = (vn * lng_ref[...] + lnb_ref[...]).astype(BF16)
    trow = lax.broadcasted_iota(jnp.int32, (ch, ch), 0)
    tcol = lax.broadcasted_iota(jnp.int32, (ch, ch), 1)
    for g in range(SGU_GROUPS):
        w = jnp.where(tcol <= trow, ws_ref[g], 0.0).astype(BF16)
        cols = slice(g * gd, (g + 1) * gd)
        for c in range(tm // ch):
            rows = slice(c * ch, (c + 1) * ch)
            mixed = jnp.dot(w, vn[rows, cols], preferred_element_type=F32) + bias_ref[:, cols]
            ysgu_ref[rows, cols] = (gu[rows, cols] * mixed).astype(BF16)
    a = jnp.dot(yatt_ref[...], wba_ref[...], preferred_element_type=F32)
    bsg = jnp.dot(ysgu_ref[...], wbs_ref[...], preferred_element_type=F32)
    merged = (jax.nn.sigmoid(ga_ref[...].astype(F32)) * a
              + jax.nn.sigmoid(gb_ref[...].astype(F32)) * bsg)
    h = x_ref[...] + jnp.dot(merged.astype(BF16), wout_ref[...], preferred_element_type=F32)
    h_ref[...] = h
    hn_ref[...] = pltpu.bitcast(_rms(h, gffn_ref[...]).astype(BF16), jnp.uint32)


def _const_spec(shape):
    return pl.BlockSpec(shape, lambda *_: (0,) * len(shape), pipeline_mode=pl.Buffered(1))


def _tail(proj, yatt, x, ln_g, ln_b, w_s, bias_full, w_ba, w_bs, w_out, g_ffn, tm=256):
    t, d = x.shape
    wide, narrow = d, SGU_WIDTH
    return pl.pallas_call(
        _tail_kernel,
        grid=(t // tm,),
        in_specs=[
            pl.BlockSpec((tm, wide), lambda i: (i, COL_GATE_A // wide)),
            pl.BlockSpec((tm, wide), lambda i: (i, COL_GATE_B // wide)),
            pl.BlockSpec((tm, narrow), lambda i: (i, COL_U // narrow)),
            pl.BlockSpec((tm, narrow), lambda i: (i, COL_VG // narrow)),
            pl.BlockSpec((tm, ATT_WIDTH), lambda i: (i, 0)),
            pl.BlockSpec((tm, d), lambda i: (i, 0)),
            _const_spec((1, narrow)),
            _const_spec((1, narrow)),
            _const_spec(w_s.shape),
            _const_spec(bias_full.shape),
            _const_spec(w_ba.shape),
            _const_spec(w_bs.shape),
            _const_spec(w_out.shape),
            _const_spec((1, d)),
        ],
        out_specs=[pl.BlockSpec((tm, d), lambda i: (i, 0)), pl.BlockSpec((tm // 2, d), lambda i: (i, 0))],
        out_shape=[jax.ShapeDtypeStruct((t, d), F32), jax.ShapeDtypeStruct((t // 2, d), jnp.uint32)],
        scratch_shapes=[pltpu.VMEM((tm, narrow), BF16)],
        compiler_params=_params(("parallel",)),
        name="mixer_tail",
    )(proj, proj, proj, proj, yatt, x, ln_g, ln_b, w_s, bias_full, w_ba, w_bs, w_out, g_ffn)


N_EXTRACT = PEER_TOPK + 1
CAND_PAIRS = tuple((a, b) for a in range(N_EXTRACT) for b in range(N_EXTRACT) if (a + 1) * (b + 1) <= N_EXTRACT)
CAND_ROWS = -(-len(CAND_PAIRS) // 8) * 8


def _extract_top(work, n_iter, n_pad):
    del n_pad
    rows = work.shape[0]
    row = lax.broadcasted_iota(jnp.int32, work.shape, 0)
    vals = []
    for _ in range(n_iter):
        m = jnp.max(work, axis=0, keepdims=True)
        idx = jnp.min(jnp.where(work == m, row, rows), axis=0, keepdims=True)
        work = jnp.where(row == idx, -jnp.inf, work)
        vals.append(m)
    return vals, jnp.zeros_like(vals[0])


def _extract_top_distinct(work, n_iter, n_pad):
    vals = []
    for _ in range(n_iter):
        m = jnp.max(work, axis=0, keepdims=True)
        work = jnp.where(work == m, -jnp.inf, work)
        vals.append(m)
    removed = jnp.sum(jnp.where(work == -jnp.inf, 1.0, 0.0), axis=0, keepdims=True)
    return vals, jnp.abs(removed - (n_iter + n_pad))


def _route_kernel(hn_ref, wqt_ref, keys_ref, tq_ref, e0_ref, s1_ref, e1_ref, cand_ref):
    hn = _unpack(hn_ref[...])
    tm = hn.shape[0]
    cand_ref[...] = jnp.full(cand_ref.shape, -jnp.inf, F32)

    def stats(s0, s1, extract):
        tv0, bad0 = extract(s0, N_EXTRACT, 0)
        tv1, bad1 = extract(s1, N_EXTRACT, 0)
        for r, (a, b) in enumerate(CAND_PAIRS):
            cand_ref[r:r + 1, :] = tv0[a] + tv1[b]
        c, badc = extract(cand_ref[...], N_EXTRACT, CAND_ROWS - len(CAND_PAIRS))
        z = jnp.ones_like(c[0])
        for r in range(1, PEER_TOPK):
            z = z + jnp.exp(c[r] - c[0])
        tau = 0.5 * (c[PEER_TOPK - 1] + c[PEER_TOPK])
        return tau - s0, jnp.exp(s0 - tv0[0]) / z, tv1[0], bad0 + bad1 + badc

    def store(ref, hh, val):
        for cc in range(tm // LANES):
            ref[cc, hh] = val[:, cc * LANES:(cc + 1) * LANES]

    def head(hh, carry):
        w = wqt_ref[pl.ds(pl.multiple_of(hh * 2 * PEER_HALF, 2 * PEER_HALF), 2 * PEER_HALF), :]
        qt = _dot_nt(w, hn).astype(BF16)
        s0 = jnp.dot(keys_ref[2 * hh], qt[:PEER_HALF], preferred_element_type=F32)
        s1 = jnp.dot(keys_ref[2 * hh + 1], qt[PEER_HALF:], preferred_element_type=F32)
        tq, e0, max1, bad = stats(s0, s1, _extract_top_distinct)
        store(tq_ref, hh, tq)
        store(e0_ref, hh, e0)
        store(s1_ref, hh, s1)
        store(e1_ref, hh, jnp.exp(s1 - max1))

        @pl.when(jnp.max(bad) > 0.0)
        def _():
            tq, e0, _, _ = stats(s0, s1, _extract_top)
            store(tq_ref, hh, tq)
            store(e0_ref, hh, e0)

        return carry

    lax.fori_loop(0, PEER_HEADS, head, 0)


def _route(hn, wq_t, keys, tm=512):
    t, d = 2 * hn.shape[0], hn.shape[1]
    nck = tm // LANES
    out_block = pl.BlockSpec((nck, PEER_HEADS, PEER_N_KEYS, LANES), lambda i: (i, 0, 0, 0))
    out_shape = jax.ShapeDtypeStruct((t // LANES, PEER_HEADS, PEER_N_KEYS, LANES), F32)
    return pl.pallas_call(
        _route_kernel,
        grid=(t // tm,),
        in_specs=[
            pl.BlockSpec((tm // 2, d), lambda i: (i, 0)),
            _const_spec(wq_t.shape),
            _const_spec(keys.shape),
        ],
        out_specs=[out_block] * 4,
        out_shape=[out_shape] * 4,
        scratch_shapes=[pltpu.VMEM((CAND_ROWS, tm), F32)],
        compiler_params=_params(("parallel",)),
        name="peer_route",
    )(hn, wq_t, keys)


def _experts_kernel(hn_ref, down_ref, upt_ref, tq_ref, e0_ref, s1_ref, e1_ref, o_ref,
                    acc_ref, hid0_ref, hid1_ref, gh0_ref, gh1_ref, *, n_tiles):
    s = pl.program_id(0)
    te = 2 * down_ref.shape[0]
    tm = 2 * hn_ref.shape[0]

    @pl.when(s == 0)
    def _():
        for ref in (hid0_ref, hid1_ref, gh0_ref, gh1_ref):
            ref[...] = jnp.zeros_like(ref)

    @pl.when((s == 0) | (s % n_tiles == 2 % n_tiles))
    def _():
        acc_ref[...] = jnp.zeros_like(acc_ref)

    def gated_tile(hid_r, gh_w, ii, cc, part):
        n_sub = PEER_N_KEYS // GATE_TILE_PARTS
        sub = slice(part * n_sub, (part + 1) * n_sub)
        rows = slice(ii * PEER_N_KEYS + part * n_sub, ii * PEER_N_KEYS + (part + 1) * n_sub)
        lanes = slice(cc * LANES, (cc + 1) * LANES)
        gate = jnp.zeros((n_sub, LANES), F32)
        for hh in range(PEER_HEADS):
            thr = tq_ref[cc, hh, ii:ii + 1, :]
            scale = e0_ref[cc, hh, ii:ii + 1, :]
            gate = gate + jnp.where(s1_ref[cc, hh, sub, :] >= thr, e1_ref[cc, hh, sub, :], 0.0) * scale
        gh_w[rows, lanes] = (gate * _gelu(hid_r[rows, lanes])).astype(BF16)

    def step(hid_w, hid_r, gh_w, gh_r):
        tiles = [(ii, cc) for ii in range(te // PEER_N_KEYS) for cc in range(tm // LANES)]
        n_m = EXPERT_M_PIECES
        n_k = len(tiles) // (2 * n_m)
        d = acc_ref.shape[0]
        tile = iter(tiles)
        for nh in range(2):
            lanes = slice(nh * (tm // 2), (nh + 1) * (tm // 2))
            words_t = slice(nh * (tm // 4), (nh + 1) * (tm // 4))
            for mh in range(n_m):
                rows_c = slice(mh * (d // n_m), (mh + 1) * (d // n_m))
                words_c = slice(mh * (d // n_m // 2), (mh + 1) * (d // n_m // 2))
                rows_a = slice(mh * (te // n_m), (mh + 1) * (te // n_m))
                words_a = slice(mh * (te // n_m // 2), (mh + 1) * (te // n_m // 2))
                up = down = None
                for kc in range(n_k):
                    ii, cc = next(tile)
                    gated_tile(hid_r, gh_w, ii, cc, 0)
                    kk_a = slice(kc * (d // n_k), (kc + 1) * (d // n_k))
                    part = _dot_nt(_unpack(down_ref[words_a, kk_a]), _unpack(hn_ref[words_t, kk_a]))
                    down = part if down is None else down + part
                    gated_tile(hid_r, gh_w, ii, cc, 1)
                    kk_c = slice(kc * (te // n_k), (kc + 1) * (te // n_k))
                    part = jnp.dot(_unpack(upt_ref[words_c, kk_c]), gh_r[kk_c, lanes],
                                   preferred_element_type=F32)
                    up = part if up is None else up + part
                hid_w[rows_a, lanes] = down
                acc_ref[rows_c, lanes] += up

    @pl.when(s % 2 == 0)
    def _():
        step(hid0_ref, hid1_ref, gh1_ref, gh0_ref)

    @pl.when(s % 2 == 1)
    def _():
        step(hid1_ref, hid0_ref, gh0_ref, gh1_ref)

    @pl.when((s >= 2) & ((s - 2) % n_tiles == n_tiles - 1))
    def _():
        o_ref[...] = acc_ref[...].T


def _experts(hn, down, up_t, tq, e0, s1, e1, tm=512, te=1024):
    t, d = 2 * hn.shape[0], hn.shape[1]
    nt, ne = t // tm, 2 * down.shape[0] // te
    nck = tm // LANES
    ni = te // PEER_N_KEYS
    assert ni % 8 == 0
    tok_a = lambda s: jnp.minimum(s // ne, nt - 1)
    tok_b = lambda s: jnp.minimum(jnp.maximum(s - 1, 0) // ne, nt - 1)
    tok_c = lambda s: jnp.minimum(jnp.maximum(s - 2, 0) // ne, nt - 1)
    exp_b = lambda s: jnp.maximum(s - 1, 0) % ne
    exp_c = lambda s: jnp.maximum(s - 2, 0) % ne
    row_block = pl.BlockSpec((nck, PEER_HEADS, ni, LANES), lambda s: (tok_b(s), 0, exp_b(s), 0))
    full_block = pl.BlockSpec((nck, PEER_HEADS, PEER_N_KEYS, LANES), lambda s: (tok_b(s), 0, 0, 0))
    return pl.pallas_call(
        functools.partial(_experts_kernel, n_tiles=ne),
        grid=(nt * ne + 2,),
        in_specs=[
            pl.BlockSpec((tm // 2, d), lambda s: (tok_a(s), 0)),
            pl.BlockSpec((te // 2, d), lambda s: (s % ne, 0)),
            pl.BlockSpec((d // 2, te), lambda s: (0, exp_c(s))),
            row_block, row_block, full_block, full_block,
        ],
        out_specs=pl.BlockSpec((tm, d), lambda s: (tok_c(s), 0)),
        out_shape=jax.ShapeDtypeStruct((t, d), F32),
        scratch_shapes=[
            pltpu.VMEM((d, tm), F32),
            pltpu.VMEM((te, tm), F32), pltpu.VMEM((te, tm), F32),
            pltpu.VMEM((te, tm), BF16), pltpu.VMEM((te, tm), BF16),
        ],
        compiler_params=_params(("arbitrary",)),
        name="peer_experts",
    )(hn, down, up_t, tq, e0, s1, e1)


def _ple_kernel(h_ref, y_ref, p_ref, gple_ref, wgate_ref, wproj_ref, gfin_ref, o_ref, *, final):
    h = h_ref[...] + y_ref[...]
    gate = jax.nn.sigmoid(jnp.dot(_rms(h, gple_ref[...]).astype(BF16), wgate_ref[...], preferred_element_type=F32))
    emb = jnp.dot(p_ref[...].astype(BF16), wproj_ref[...], preferred_element_type=F32)
    h = h + gate * emb
    o_ref[...] = _rms(h, gfin_ref[...]) if final else h


def _ple(h, y, p, g_ple, w_gate, w_proj, g_final, final, tm=512):
    t, d = h.shape
    pd = p.shape[1]
    return pl.pallas_call(
        functools.partial(_ple_kernel, final=final),
        grid=(t // tm,),
        in_specs=[
            pl.BlockSpec((tm, d), lambda i: (i, 0)),
            pl.BlockSpec((tm, d), lambda i: (i, 0)),
            pl.BlockSpec((tm, pd), lambda i: (i, 0)),
            _const_spec((1, d)),
            _const_spec(w_gate.shape),
            _const_spec(w_proj.shape),
            _const_spec((1, d)),
        ],
        out_specs=pl.BlockSpec((tm, d), lambda i: (i, 0)),
        out_shape=jax.ShapeDtypeStruct((t, d), F32),
        compiler_params=_params(("parallel",)),
        name="ple",
    )(h, y, p, g_ple, w_gate, w_proj, g_final)


def kernel(x, p, norm_mix_g, w_in, sgu_ln_g, sgu_ln_b, sgu_w, sgu_b, w_branch_attn, w_branch_sgu, w_out, norm_ffn_g, peer_w_query, peer_sub_keys, peer_down, peer_up, norm_ple_g, ple_w_proj, ple_w_gate, final_norm_g):
    b, s, d = x.shape
    t = b * s
    depth = w_in.shape[0]
    h = x.reshape(t, d)
    row = lambda v: v.reshape(1, -1)
    aw = ATT_WIDTH
    for i in range(depth):
        wi = w_in[i]
        w_perm = jnp.concatenate([wi[:, 3 * aw + 2 * SGU_WIDTH:], wi[:, :3 * aw + 2 * SGU_WIDTH]], axis=1).astype(BF16)
        proj = _inproj(h, row(norm_mix_g[i]), w_perm)
        yatt = _moba(proj.reshape(b, s, -1)).reshape(t, aw)
        bias_full = jnp.repeat(sgu_b[i].T, SGU_GROUP_DIM, axis=1)
        h, hn = _tail(proj, yatt, h, row(sgu_ln_g[i]), row(sgu_ln_b[i]), sgu_w[i], bias_full,
                      w_branch_attn[i].astype(BF16), w_branch_sgu[i].astype(BF16), w_out[i].astype(BF16),
                      row(norm_ffn_g[i]))
        keys = peer_sub_keys[i].reshape(2 * PEER_HEADS, PEER_N_KEYS, PEER_HALF).astype(BF16)
        tq, e0, s1, e1 = _route(hn, peer_w_query[i].T.astype(BF16), keys)
        y = _experts(hn, _pack_bf16(peer_down[i]), _pack_bf16(peer_up[i], transpose=True), tq, e0, s1, e1)
        h = _ple(h, y, p[i].reshape(t, -1), row(norm_ple_g[i]), ple_w_gate[i].astype(BF16),
                 ple_w_proj[i].astype(BF16), row(final_norm_g), final=(i == depth - 1))
    if depth == 0:
        raise ValueError("depth must be at least 1")
    return h.reshape(b, s, d)
```

```python
import functools

import jax
import jax.numpy as jnp
from jax import lax
from jax.experimental import pallas as pl
from jax.experimental.pallas import tpu as pltpu

F32 = jnp.float32
BF16 = jnp.bfloat16

NORM_EPS = 1e-6
ATT_HEADS = 8
ATT_HEAD_DIM = 128
ATT_WIDTH = ATT_HEADS * ATT_HEAD_DIM
MOBA_BLOCK = 256
MOBA_TOPK = 3
MOBA_BLOCKS_PER_STEP = 4
MOBA_HEADS_PER_STEP = 4
SGU_GROUPS = 8
SGU_GROUP_DIM = 128
SGU_WIDTH = SGU_GROUPS * SGU_GROUP_DIM
SGU_CHUNK = 128
PEER_HEADS = 8
PEER_N_KEYS = 128
PEER_HALF = 128
PEER_TOPK = 16

GATE_TILE_PARTS = 2
EXPERT_M_PIECES = 4
LANES = 128
SUBLANES = 8
VMEM_LIMIT = 56 * 1024 * 1024
MASK_NEG = -(2.0 ** 30)
LOG2E = 1.4426950408889634

COL_GATE_A = 0
COL_GATE_B = 2048
COL_Q = 4096
COL_K = COL_Q + ATT_WIDTH
COL_V = COL_K + ATT_WIDTH
COL_U = COL_V + ATT_WIDTH
COL_VG = COL_U + SGU_WIDTH


def _params(semantics, flags=None):
    return pltpu.CompilerParams(dimension_semantics=semantics, vmem_limit_bytes=VMEM_LIMIT, flags=flags)


def _rms(x, gain):
    return x * lax.rsqrt(jnp.mean(x * x, axis=-1, keepdims=True) + NORM_EPS) * gain


def _gelu(x):
    return 0.5 * x * (1.0 + lax.erf(x * (0.5 ** 0.5)))


def _dot_nt(a, b):
    return lax.dot_general(a, b, (((1,), (1,)), ((), ())), preferred_element_type=F32)


def _unpack(words):
    return pltpu.bitcast(words, BF16)


def _pack_kernel(x_ref, o_ref, *, transpose):
    x = x_ref[...]
    if transpose:
        x = x.T
    o_ref[...] = pltpu.bitcast(x.astype(BF16), jnp.uint32)


def _pack_bf16(x, transpose=False, tr=512):
    r, c = x.shape
    if transpose:
        out_shape, out_block, out_map = (c // 2, r), (c // 2, tr), lambda i: (0, i)
    else:
        out_shape, out_block, out_map = (r // 2, c), (tr // 2, c), lambda i: (i, 0)
    return pl.pallas_call(
        functools.partial(_pack_kernel, transpose=transpose),
        grid=(r // tr,),
        in_specs=[pl.BlockSpec((tr, c), lambda i: (i, 0))],
        out_specs=pl.BlockSpec(out_block, out_map),
        out_shape=jax.ShapeDtypeStruct(out_shape, jnp.uint32),
        compiler_params=_params(("parallel",)),
        name="pack_bf16_t" if transpose else "pack_bf16",
    )(x)


def _inproj_kernel(x_ref, g_ref, w_ref, o_ref, xn_ref):
    @pl.when(pl.program_id(1) == 0)
    def _():
        xn_ref[...] = _rms(x_ref[...], g_ref[...]).astype(BF16)

    o_ref[...] = jnp.dot(xn_ref[...], w_ref[...], preferred_element_type=F32).astype(o_ref.dtype)


def _inproj(x, gain, w, tm=1024, tn=512):
    t, d = x.shape
    n = w.shape[1]
    return pl.pallas_call(
        _inproj_kernel,
        grid=(t // tm, n // tn),
        in_specs=[
            pl.BlockSpec((tm, d), lambda i, j: (i, 0)),
            pl.BlockSpec((1, d), lambda i, j: (0, 0)),
            pl.BlockSpec((d, tn), lambda i, j: (0, j)),
        ],
        out_specs=pl.BlockSpec((tm, tn), lambda i, j: (i, j)),
        out_shape=jax.ShapeDtypeStruct((t, n), BF16),
        scratch_shapes=[pltpu.VMEM((tm, d), BF16)],
        compiler_params=_params(("parallel", "arbitrary")),
        name="inproj",
    )(x, gain, w)


def _moba_kernel(q_ref, k_ref, v_ref, o_ref, kmean_ref, vt_ref, mask_ref):
    j = pl.program_id(2)
    blk, hd, grp = MOBA_BLOCK, ATT_HEAD_DIM, MOBA_BLOCKS_PER_STEP
    heads = range(q_ref.shape[1] // hd)
    nb = k_ref.shape[0] // blk
    nbp = kmean_ref.shape[1]
    c_exp = (hd ** -0.5) * LOG2E

    def col(h):
        return slice(h * hd, (h + 1) * hd)

    @pl.when(j == 0)
    def _():
        kmean_ref[...] = jnp.zeros_like(kmean_ref)

        def body(g, c):
            for u in range(grp):
                n = g * grp + u
                rows = pl.ds(pl.multiple_of(n * blk, blk), blk)
                for h in heads:
                    kmean_ref[h, pl.ds(n, 1), :] = (jnp.sum(k_ref[rows, col(h)].astype(F32), axis=0, keepdims=True)
                                                    * (1.0 / blk))
                    vt_ref[h, g, :, u * blk:(u + 1) * blk] = v_ref[rows, col(h)].T
            return c

        lax.fori_loop(0, nb // grp, body, 0)

    own = pl.ds(pl.multiple_of(j * blk, blk), blk)
    qs = [q_ref[:, col(h)] for h in heads]
    gates = [_dot_nt(kmean_ref[h].astype(BF16), qs[h]) for h in heads]
    t_own = [_dot_nt(k_ref[own, col(h)], qs[h]) * c_exp for h in heads]

    row = lax.broadcasted_iota(jnp.int32, gates[0].shape, 0)
    for h in heads:
        g = jnp.where(row < j, gates[h], -jnp.inf)
        maskneg = jnp.full(g.shape, MASK_NEG, F32)
        for r in range(MOBA_TOPK):
            m = jnp.max(g, axis=0, keepdims=True)
            idx = jnp.min(jnp.where(g == m, row, nbp), axis=0, keepdims=True)
            idx = jnp.where(r < j, idx, -1)
            hit = row == idx
            maskneg = jnp.where(hit, 0.0, maskneg)
            g = jnp.where(hit, -jnp.inf, g)
        mask_ref[h] = maskneg

    krow = lax.broadcasted_iota(jnp.int32, (blk, blk), 0)
    qcol = lax.broadcasted_iota(jnp.int32, (blk, blk), 1)
    state = []
    for h in heads:
        t = jnp.where(krow <= qcol, t_own[h], -jnp.inf)
        m0 = jnp.max(t, axis=0, keepdims=True)
        p = jnp.exp2(t - m0)
        l0 = jnp.sum(p, axis=0, keepdims=True)
        acc0 = jnp.dot(v_ref[own, col(h)].T, p.astype(BF16), preferred_element_type=F32)
        state.append((m0, l0, acc0))

    def attend(g, carry):
        rows = pl.ds(pl.multiple_of(g * (grp * blk), grp * blk), grp * blk)
        scores = [_dot_nt(k_ref[rows, col(h)], qs[h]) * c_exp for h in heads]
        new = []
        for h in heads:
            m_prev, l_prev, acc = carry[h]
            ts = [scores[h][u * blk:(u + 1) * blk] + mask_ref[h, pl.ds(g * grp + u, 1), :] for u in range(grp)]
            m_new = m_prev
            for t in ts:
                m_new = jnp.maximum(m_new, jnp.max(t, axis=0, keepdims=True))
            alpha = jnp.exp2(m_prev - m_new)
            l_new = alpha * l_prev
            ps = []
            for t in ts:
                p = jnp.exp2(t - m_new)
                l_new = l_new + jnp.sum(p, axis=0, keepdims=True)
                ps.append(p.astype(BF16))
            acc = alpha * acc + jnp.dot(vt_ref[h, g], jnp.concatenate(ps, axis=0), preferred_element_type=F32)
            new.append((m_new, l_new, acc))
        return tuple(new)

    final = lax.fori_loop(0, (j + grp - 1) // grp, attend, tuple(state))
    for h in heads:
        _, l_fin, acc = final[h]
        o_ref[:, col(h)] = (acc / l_fin).T.astype(o_ref.dtype)


def _moba(proj3):
    b, s, _ = proj3.shape
    blk, hd, grp, nh = MOBA_BLOCK, ATT_HEAD_DIM, MOBA_BLOCKS_PER_STEP, MOBA_HEADS_PER_STEP
    nb = s // blk
    assert s % blk == 0 and nb % grp == 0 and ATT_HEADS % nh == 0
    nbp = -(-nb // 16) * 16
    wide = nh * hd
    qb, kb, vb = COL_Q // wide, COL_K // wide, COL_V // wide
    return pl.pallas_call(
        _moba_kernel,
        grid=(b, ATT_HEADS // nh, nb),
        in_specs=[
            pl.BlockSpec((None, blk, wide), lambda bi, h, j: (bi, j, qb + h)),
            pl.BlockSpec((None, s, wide), lambda bi, h, j: (bi, 0, kb + h)),
            pl.BlockSpec((None, s, wide), lambda bi, h, j: (bi, 0, vb + h)),
        ],
        out_specs=pl.BlockSpec((None, blk, wide), lambda bi, h, j: (bi, j, h)),
        out_shape=jax.ShapeDtypeStruct((b, s, ATT_WIDTH), BF16),
        scratch_shapes=[
            pltpu.VMEM((nh, nbp, hd), F32),
            pltpu.VMEM((nh, nb // grp, hd, grp * blk), BF16),
            pltpu.VMEM((nh, nbp, blk), F32),
        ],
        compiler_params=_params(("parallel", "parallel", "arbitrary")),
        name="moba",
    )(proj3, proj3, proj3)


def _tail_kernel(ga_ref, gb_ref, u_ref, vg_ref, yatt_ref, x_ref, lng_ref, lnb_ref, ws_ref, bias_ref,
                 wba_ref, wbs_ref, wout_ref, gffn_ref, h_ref, hn_ref, ysgu_ref):
    tm = u_ref.shape[0]
    ch, gd = SGU_CHUNK, SGU_GROUP_DIM
    gu = _gelu(u_ref[...].astype(F32))
    gv = _gelu(vg_ref[...].astype(F32))
    mu = jnp.mean(gv, axis=-1, keepdims=True)
    xc = gv - mu
    vn = xc * lax.rsqrt(jnp.mean(xc * xc, axis=-1, keepdims=True) + NORM_EPS)
    vn = (vn * lng_ref[...] + lnb_ref[...]).astype(BF16)
    trow = lax.broadcasted_iota(jnp.int32, (ch, ch), 0)
    tcol = lax.broadcasted_iota(jnp.int32, (ch, ch), 1)
    for g in range(SGU_GROUPS):
        w = jnp.where(tcol <= trow, ws_ref[g], 0.0).astype(BF16)
        cols = slice(g * gd, (g + 1) * gd)
        for c in range(tm // ch):
            rows = slice(c * ch, (c + 1) * ch)
            mixed = jnp.dot(w, vn[rows, cols], preferred_element_type=F32) + bias_ref[:, cols]
            ysgu_ref[rows, cols] = (gu[rows, cols] * mixed).astype(BF16)
    a = jnp.dot(yatt_ref[...], wba_ref[...], preferred_element_type=F32)
    bsg = jnp.dot(ysgu_ref[...], wbs_ref[...], preferred_element_type=F32)
    merged = (jax.nn.sigmoid(ga_ref[...].astype(F32)) * a
              + jax.nn.sigmoid(gb_ref[...].astype(F32)) * bsg)
    h = x_ref[...] + jnp.dot(merged.astype(BF16), wout_ref[...], preferred_element_type=F32)
    h_ref[...] = h
    hn_ref[...] = pltpu.bitcast(_rms(h, gffn_ref[...]).astype(BF16), jnp.uint32)


def _const_spec(shape):
    return pl.BlockSpec(shape, lambda *_: (0,) * len(shape), pipeline_mode=pl.Buffered(1))


def _tail(proj, yatt, x, ln_g, ln_b, w_s, bias_full, w_ba, w_bs, w_out, g_ffn, tm=256):
    t, d = x.shape
    wide, narrow = d, SGU_WIDTH
    return pl.pallas_call(
        _tail_kernel,
        grid=(t // tm,),
        in_specs=[
            pl.BlockSpec((tm, wide), lambda i: (i, COL_GATE_A // wide)),
            pl.BlockSpec((tm, wide), lambda i: (i, COL_GATE_B // wide)),
            pl.BlockSpec((tm, narrow), lambda i: (i, COL_U // narrow)),
            pl.BlockSpec((tm, narrow), lambda i: (i, COL_VG // narrow)),
            pl.BlockSpec((tm, ATT_WIDTH), lambda i: (i, 0)),
            pl.BlockSpec((tm, d), lambda i: (i, 0)),
            _const_spec((1, narrow)),
            _const_spec((1, narrow)),
            _const_spec(w_s.shape),
            _const_spec(bias_full.shape),
            _const_spec(w_ba.shape),
            _const_spec(w_bs.shape),
            _const_spec(w_out.shape),
            _const_spec((1, d)),
        ],
        out_specs=[pl.BlockSpec((tm, d), lambda i: (i, 0)), pl.BlockSpec((tm // 2, d), lambda i: (i, 0))],
        out_shape=[jax.ShapeDtypeStruct((t, d), F32), jax.ShapeDtypeStruct((t // 2, d), jnp.uint32)],
        scratch_shapes=[pltpu.VMEM((tm, narrow), BF16)],
        compiler_params=_params(("parallel",)),
        name="mixer_tail",
    )(proj, proj, proj, proj, yatt, x, ln_g, ln_b, w_s, bias_full, w_ba, w_bs, w_out, g_ffn)


N_EXTRACT = PEER_TOPK + 1
CAND_PAIRS = tuple((a, b) for a in range(N_EXTRACT) for b in range(N_EXTRACT) if (a + 1) * (b + 1) <= N_EXTRACT)
CAND_ROWS = SUBLANES * (1 << (-(-len(CAND_PAIRS) // SUBLANES) - 1).bit_length())
ROUTE_HEADS_PER_STEP = 2


def _extract_top(works, n_iter, n_pad):
    del n_pad
    rows = works[0].shape[0]
    row = lax.broadcasted_iota(jnp.int32, works[0].shape, 0)
    works = list(works)
    vals = [[] for _ in works]
    for _ in range(n_iter):
        for k, work in enumerate(works):
            m = jnp.max(work, axis=0, keepdims=True)
            idx = jnp.min(jnp.where(work == m, row, rows), axis=0, keepdims=True)
            works[k] = jnp.where(row == idx, -jnp.inf, work)
            vals[k].append(m)
    return [(v, jnp.zeros_like(v[0])) for v in vals]


def _sorting_network(n):
    pairs = []
    p = 1
    while p < n:
        k = p
        while k >= 1:
            for j in range(k % p, n - k, 2 * k):
                for i in range(min(k, n - j - k)):
                    if (i + j) // (2 * p) == (i + j + k) // (2 * p):
                        pairs.append((i + j, i + j + k))
            k //= 2
        p *= 2
    return pairs


def _extract_top_distinct(works, n_iter, n_pad):
    del n_pad
    lists = []
    for work in works:
        slabs = [work[SUBLANES * k:SUBLANES * (k + 1)] for k in range(work.shape[0] // SUBLANES)]
        for i, j in _sorting_network(len(slabs)):
            slabs[i], slabs[j] = jnp.maximum(slabs[i], slabs[j]), jnp.minimum(slabs[i], slabs[j])
        lists.append(slabs)
    vals = [[] for _ in works]
    popped = [jnp.zeros_like(slabs[0]) for slabs in lists]
    for r in range(n_iter):
        for k, slabs in enumerate(lists):
            m = jnp.max(slabs[0], axis=0, keepdims=True)
            hit = slabs[0] == m
            for d in range(min(len(slabs), n_iter - r - 1)):
                below = slabs[d + 1] if d + 1 < len(slabs) else -jnp.inf
                slabs[d] = jnp.where(hit, below, slabs[d])
            popped[k] = popped[k] + jnp.where(hit, 1.0, 0.0)
            vals[k].append(m)
    return [(v, jnp.abs(jnp.sum(p, axis=0, keepdims=True) - n_iter)) for v, p in zip(vals, popped)]


def _route_kernel(hn_ref, wqt_ref, keys_ref, tq_ref, e0_ref, s1_ref, e1_ref, cand_ref):
    hn = _unpack(hn_ref[...])
    tm = hn.shape[0]
    group = cand_ref.shape[0]
    cand_ref[...] = jnp.full(cand_ref.shape, -jnp.inf, F32)

    def stats(scores, extract):
        tops = extract([s for pair in scores for s in pair], N_EXTRACT, 0)
        for k in range(group):
            tv0, tv1 = tops[2 * k][0], tops[2 * k + 1][0]
            for r, (a, b) in enumerate(CAND_PAIRS):
                cand_ref[k, r:r + 1, :] = tv0[a] + tv1[b]
        ctops = extract([cand_ref[k] for k in range(group)], N_EXTRACT, CAND_ROWS - len(CAND_PAIRS))
        out = []
        for k, (s0, _) in enumerate(scores):
            c, badc = ctops[k]
            z = jnp.ones_like(c[0])
            for r in range(1, PEER_TOPK):
                z = z + jnp.exp(c[r] - c[0])
            tau = 0.5 * (c[PEER_TOPK - 1] + c[PEER_TOPK])
            bad = tops[2 * k][1] + tops[2 * k + 1][1] + badc
            out.append((tau - s0, jnp.exp(s0 - tops[2 * k][0][0]) / z, tops[2 * k + 1][0][0], bad))
        return out

    def store(ref, hh, val):
        for cc in range(tm // LANES):
            ref[cc, hh] = val[:, cc * LANES:(cc + 1) * LANES]

    def heads(g, carry):
        scores = []
        for k in range(group):
            hh = g * group + k
            w = wqt_ref[pl.ds(pl.multiple_of(hh * 2 * PEER_HALF, 2 * PEER_HALF), 2 * PEER_HALF), :]
            qt = _dot_nt(w, hn).astype(BF16)
            scores.append((jnp.dot(keys_ref[2 * hh], qt[:PEER_HALF], preferred_element_type=F32),
                           jnp.dot(keys_ref[2 * hh + 1], qt[PEER_HALF:], preferred_element_type=F32)))
        fast = stats(scores, _extract_top_distinct)
        bad = None
        for k, (tq, e0, max1, bad_k) in enumerate(fast):
            hh = g * group + k
            store(tq_ref, hh, tq)
            store(e0_ref, hh, e0)
            store(s1_ref, hh, scores[k][1])
            store(e1_ref, hh, jnp.exp(scores[k][1] - max1))
            bad = bad_k if bad is None else bad + bad_k

        @pl.when(jnp.max(bad) > 0.0)
        def _():
            for k, (tq, e0, _, _) in enumerate(stats(scores, _extract_top)):
                store(tq_ref, g * group + k, tq)
                store(e0_ref, g * group + k, e0)

        return carry

    lax.fori_loop(0, PEER_HEADS // group, heads, 0)


def _route(hn, wq_t, keys, tm=512):
    t, d = 2 * hn.shape[0], hn.shape[1]
    nck = tm // LANES
    out_block = pl.BlockSpec((nck, PEER_HEADS, PEER_N_KEYS, LANES), lambda i: (i, 0, 0, 0))
    out_shape = jax.ShapeDtypeStruct((t // LANES, PEER_HEADS, PEER_N_KEYS, LANES), F32)
    return pl.pallas_call(
        _route_kernel,
        grid=(t // tm,),
        in_specs=[
            pl.BlockSpec((tm // 2, d), lambda i: (i, 0)),
            _const_spec(wq_t.shape),
            _const_spec(keys.shape),
        ],
        out_specs=[out_block] * 4,
        out_shape=[out_shape] * 4,
        scratch_shapes=[pltpu.VMEM((ROUTE_HEADS_PER_STEP, CAND_ROWS, tm), F32)],
        compiler_params=_params(("parallel",)),
        name="peer_route",
    )(hn, wq_t, keys)


def _experts_kernel(hn_ref, down_ref, upt_ref, tq_ref, e0_ref, s1_ref, e1_ref, o_ref,
                    acc_ref, hid0_ref, hid1_ref, gh0_ref, gh1_ref, *, n_tiles):
    s = pl.program_id(0)
    te = 2 * down_ref.shape[0]
    tm = 2 * hn_ref.shape[0]

    @pl.when(s == 0)
    def _():
        for ref in (hid0_ref, hid1_ref, gh0_ref, gh1_ref):
            ref[...] = jnp.zeros_like(ref)

    @pl.when((s == 0) | (s % n_tiles == 2 % n_tiles))
    def _():
        acc_ref[...] = jnp.zeros_like(acc_ref)

    def gated_tile(hid_r, gh_w, ii, cc, part):
        n_sub = PEER_N_KEYS // GATE_TILE_PARTS
        sub = slice(part * n_sub, (part + 1) * n_sub)
        rows = slice(ii * PEER_N_KEYS + part * n_sub, ii * PEER_N_KEYS + (part + 1) * n_sub)
        lanes = slice(cc * LANES, (cc + 1) * LANES)
        gate = jnp.zeros((n_sub, LANES), F32)
        for hh in range(PEER_HEADS):
            thr = tq_ref[cc, hh, ii:ii + 1, :]
            scale = e0_ref[cc, hh, ii:ii + 1, :]
            gate = gate + jnp.where(s1_ref[cc, hh, sub, :] >= thr, e1_ref[cc, hh, sub, :], 0.0) * scale
        gh_w[rows, lanes] = (gate * _gelu(hid_r[rows, lanes])).astype(BF16)

    def step(hid_w, hid_r, gh_w, gh_r):
        tiles = [(ii, cc) for ii in range(te // PEER_N_KEYS) for cc in range(tm // LANES)]
        n_m = EXPERT_M_PIECES
        n_k = len(tiles) // (2 * n_m)
        d = acc_ref.shape[0]
        tile = iter(tiles)
        for nh in range(2):
            lanes = slice(nh * (tm // 2), (nh + 1) * (tm // 2))
            words_t = slice(nh * (tm // 4), (nh + 1) * (tm // 4))
            for mh in range(n_m):
                rows_c = slice(mh * (d // n_m), (mh + 1) * (d // n_m))
                words_c = slice(mh * (d // n_m // 2), (mh + 1) * (d // n_m // 2))
                rows_a = slice(mh * (te // n_m), (mh + 1) * (te // n_m))
                words_a = slice(mh * (te // n_m // 2), (mh + 1) * (te // n_m // 2))
                up = down = None
                for kc in range(n_k):
                    ii, cc = next(tile)
                    gated_tile(hid_r, gh_w, ii, cc, 0)
                    kk_a = slice(kc * (d // n_k), (kc + 1) * (d // n_k))
                    part = _dot_nt(_unpack(down_ref[words_a, kk_a]), _unpack(hn_ref[words_t, kk_a]))
                    down = part if down is None else down + part
                    gated_tile(hid_r, gh_w, ii, cc, 1)
                    kk_c = slice(kc * (te // n_k), (kc + 1) * (te // n_k))
                    part = jnp.dot(_unpack(upt_ref[words_c, kk_c]), gh_r[kk_c, lanes],
                                   preferred_element_type=F32)
                    up = part if up is None else up + part
                hid_w[rows_a, lanes] = down
                acc_ref[rows_c, lanes] += up

    @pl.when(s % 2 == 0)
    def _():
        step(hid0_ref, hid1_ref, gh1_ref, gh0_ref)

    @pl.when(s % 2 == 1)
    def _():
        step(hid1_ref, hid0_ref, gh0_ref, gh1_ref)

    @pl.when((s >= 2) & ((s - 2) % n_tiles == n_tiles - 1))
    def _():
        o_ref[...] = acc_ref[...].T


def _experts(hn, down, up_t, tq, e0, s1, e1, tm=512, te=1024):
    t, d = 2 * hn.shape[0], hn.shape[1]
    nt, ne = t // tm, 2 * down.shape[0] // te
    nck = tm // LANES
    ni = te // PEER_N_KEYS
    assert ni % 8 == 0
    tok_a = lambda s: jnp.minimum(s // ne, nt - 1)
    tok_b = lambda s: jnp.minimum(jnp.maximum(s - 1, 0) // ne, nt - 1)
    tok_c = lambda s: jnp.minimum(jnp.maximum(s - 2, 0) // ne, nt - 1)
    exp_b = lambda s: jnp.maximum(s - 1, 0) % ne
    exp_c = lambda s: jnp.maximum(s - 2, 0) % ne
    row_block = pl.BlockSpec((nck, PEER_HEADS, ni, LANES), lambda s: (tok_b(s), 0, exp_b(s), 0))
    full_block = pl.BlockSpec((nck, PEER_HEADS, PEER_N_KEYS, LANES), lambda s: (tok_b(s), 0, 0, 0))
    return pl.pallas_call(
        functools.partial(_experts_kernel, n_tiles=ne),
        grid=(nt * ne + 2,),
        in_specs=[
            pl.BlockSpec((tm // 2, d), lambda s: (tok_a(s), 0)),
            pl.BlockSpec((te // 2, d), lambda s: (s % ne, 0)),
            pl.BlockSpec((d // 2, te), lambda s: (0, exp_c(s))),
            row_block, row_block, full_block, full_block,
        ],
        out_specs=pl.BlockSpec((tm, d), lambda s: (tok_c(s), 0)),
        out_shape=jax.ShapeDtypeStruct((t, d), F32),
        scratch_shapes=[
            pltpu.VMEM((d, tm), F32),
            pltpu.VMEM((te, tm), F32), pltpu.VMEM((te, tm), F32),
            pltpu.VMEM((te, tm), BF16), pltpu.VMEM((te, tm), BF16),
        ],
        compiler_params=_params(("arbitrary",)),
        name="peer_experts",
    )(hn, down, up_t, tq, e0, s1, e1)


def _ple_kernel(h_ref, y_ref, p_ref, gple_ref, wgate_ref, wproj_ref, gfin_ref, o_ref, *, final):
    h = h_ref[...] + y_ref[...]
    gate = jax.nn.sigmoid(jnp.dot(_rms(h, gple_ref[...]).astype(BF16), wgate_ref[...], preferred_element_type=F32))
    emb = jnp.dot(p_ref[...].astype(BF16), wproj_ref[...], preferred_element_type=F32)
    h = h + gate * emb
    o_ref[...] = _rms(h, gfin_ref[...]) if final else h


def _ple(h, y, p, g_ple, w_gate, w_proj, g_final, final, tm=512):
    t, d = h.shape
    pd = p.shape[1]
    return pl.pallas_call(
        functools.partial(_ple_kernel, final=final),
        grid=(t // tm,),
        in_specs=[
            pl.BlockSpec((tm, d), lambda i: (i, 0)),
            pl.BlockSpec((tm, d), lambda i: (i, 0)),
            pl.BlockSpec((tm, pd), lambda i: (i, 0)),
            _const_spec((1, d)),
            _const_spec(w_gate.shape),
            _const_spec(w_proj.shape),
            _const_spec((1, d)),
        ],
        out_specs=pl.BlockSpec((tm, d), lambda i: (i, 0)),
        out_shape=jax.ShapeDtypeStruct((t, d), F32),
        compiler_params=_params(("parallel",)),
        name="ple",
    )(h, y, p, g_ple, w_gate, w_proj, g_final)


def kernel(x, p, norm_mix_g, w_in, sgu_ln_g, sgu_ln_b, sgu_w, sgu_b, w_branch_attn, w_branch_sgu, w_out, norm_ffn_g, peer_w_query, peer_sub_keys, peer_down, peer_up, norm_ple_g, ple_w_proj, ple_w_gate, final_norm_g):
    b, s, d = x.shape
    t = b * s
    depth = w_in.shape[0]
    h = x.reshape(t, d)
    row = lambda v: v.reshape(1, -1)
    aw = ATT_WIDTH
    for i in range(depth):
        wi = w_in[i]
        w_perm = jnp.concatenate([wi[:, 3 * aw + 2 * SGU_WIDTH:], wi[:, :3 * aw + 2 * SGU_WIDTH]], axis=1).astype(BF16)
        proj = _inproj(h, row(norm_mix_g[i]), w_perm)
        yatt = _moba(proj.reshape(b, s, -1)).reshape(t, aw)
        bias_full = jnp.repeat(sgu_b[i].T, SGU_GROUP_DIM, axis=1)
        h, hn = _tail(proj, yatt, h, row(sgu_ln_g[i]), row(sgu_ln_b[i]), sgu_w[i], bias_full,
                      w_branch_attn[i].astype(BF16), w_branch_sgu[i].astype(BF16), w_out[i].astype(BF16),
                      row(norm_ffn_g[i]))
        keys = peer_sub_keys[i].reshape(2 * PEER_HEADS, PEER_N_KEYS, PEER_HALF).astype(BF16)
        tq, e0, s1, e1 = _route(hn, peer_w_query[i].T.astype(BF16), keys)
        y = _experts(hn, _pack_bf16(peer_down[i]), _pack_bf16(peer_up[i], transpose=True), tq, e0, s1, e1)
        h = _ple(h, y, p[i].reshape(t, -1), row(norm_ple_g[i]), ple_w_gate[i].astype(BF16),
                 ple_w_proj[i].astype(BF16), row(final_norm_g), final=(i == depth - 1))
    if depth == 0:
        raise ValueError("depth must be at least 1")
    return h.reshape(b, s, d)
```

```python
import functools

import jax
import jax.numpy as jnp
from jax import lax
from jax.experimental import pallas as pl
from jax.experimental.pallas import tpu as pltpu

F32 = jnp.float32
BF16 = jnp.bfloat16

NORM_EPS = 1e-6
ATT_HEADS = 8
ATT_HEAD_DIM = 128
ATT_WIDTH = ATT_HEADS * ATT_HEAD_DIM
MOBA_BLOCK = 256
MOBA_TOPK = 3
MOBA_BLOCKS_PER_STEP = 4
MOBA_HEADS_PER_STEP = 4
SGU_GROUPS = 8
SGU_GROUP_DIM = 128
SGU_WIDTH = SGU_GROUPS * SGU_GROUP_DIM
SGU_CHUNK = 128
PEER_HEADS = 8
PEER_N_KEYS = 128
PEER_HALF = 128
PEER_TOPK = 16

GATE_TILE_PARTS = 2
EXPERT_M_PIECES = 4
LANES = 128
SUBLANES = 8
BF16_ROWS = 2 * SUBLANES
VMEM_LIMIT = 56 * 1024 * 1024
MASK_NEG = -(2.0 ** 30)
LOG2E = 1.4426950408889634

COL_GATE_A = 0
COL_GATE_B = 2048
COL_Q = 4096
COL_K = COL_Q + ATT_WIDTH
COL_V = COL_K + ATT_WIDTH
COL_U = COL_V + ATT_WIDTH
COL_VG = COL_U + SGU_WIDTH


def _params(semantics, flags=None):
    return pltpu.CompilerParams(dimension_semantics=semantics, vmem_limit_bytes=VMEM_LIMIT, flags=flags)


def _rms(x, gain):
    return x * lax.rsqrt(jnp.mean(x * x, axis=-1, keepdims=True) + NORM_EPS) * gain


def _gelu(x):
    return 0.5 * x * (1.0 + lax.erf(x * (0.5 ** 0.5)))


def _dot_nt(a, b):
    return lax.dot_general(a, b, (((1,), (1,)), ((), ())), preferred_element_type=F32)


def _unpack(words):
    return pltpu.bitcast(words, BF16)


def _pack_kernel(x_ref, o_ref, *, transpose):
    x = x_ref[...]
    if transpose:
        x = x.T
    o_ref[...] = pltpu.bitcast(x.astype(BF16), jnp.uint32)


def _pack_bf16(x, transpose=False, tr=512):
    r, c = x.shape
    if transpose:
        out_shape, out_block, out_map = (c // 2, r), (c // 2, tr), lambda i: (0, i)
    else:
        out_shape, out_block, out_map = (r // 2, c), (tr // 2, c), lambda i: (i, 0)
    return pl.pallas_call(
        functools.partial(_pack_kernel, transpose=transpose),
        grid=(r // tr,),
        in_specs=[pl.BlockSpec((tr, c), lambda i: (i, 0))],
        out_specs=pl.BlockSpec(out_block, out_map),
        out_shape=jax.ShapeDtypeStruct(out_shape, jnp.uint32),
        compiler_params=_params(("parallel",)),
        name="pack_bf16_t" if transpose else "pack_bf16",
    )(x)


def _inproj_kernel(x_ref, g_ref, w_ref, o_ref, xn_ref):
    @pl.when(pl.program_id(1) == 0)
    def _():
        xn_ref[...] = _rms(x_ref[...], g_ref[...]).astype(BF16)

    o_ref[...] = jnp.dot(xn_ref[...], w_ref[...], preferred_element_type=F32).astype(o_ref.dtype)


def _inproj(x, gain, w, col_shift, tm=1024, tn=512):
    t, d = x.shape
    n = w.shape[1]
    assert col_shift % tn == 0
    shift = col_shift // tn
    return pl.pallas_call(
        _inproj_kernel,
        grid=(t // tm, n // tn),
        in_specs=[
            pl.BlockSpec((tm, d), lambda i, j: (i, 0)),
            pl.BlockSpec((1, d), lambda i, j: (0, 0)),
            pl.BlockSpec((d, tn), lambda i, j: (0, (j + shift) % (n // tn))),
        ],
        out_specs=pl.BlockSpec((tm, tn), lambda i, j: (i, j)),
        out_shape=jax.ShapeDtypeStruct((t, n), BF16),
        scratch_shapes=[pltpu.VMEM((tm, d), BF16)],
        compiler_params=_params(("parallel", "arbitrary")),
        name="inproj",
    )(x, gain, w)


def _moba_kernel(q_ref, k_ref, v_ref, o_ref, kmean_ref, vt_ref, mask_ref):
    j = pl.program_id(2)
    blk, hd, grp = MOBA_BLOCK, ATT_HEAD_DIM, MOBA_BLOCKS_PER_STEP
    heads = range(q_ref.shape[1] // hd)
    nb = k_ref.shape[0] // blk
    nbp = kmean_ref.shape[1]
    ones_rows = vt_ref.shape[2] - hd
    c_exp = (hd ** -0.5) * LOG2E

    def col(h):
        return slice(h * hd, (h + 1) * hd)

    @pl.when(j == 0)
    def _():
        kmean_ref[...] = jnp.zeros_like(kmean_ref)

        def body(g, c):
            for u in range(grp):
                n = g * grp + u
                rows = pl.ds(pl.multiple_of(n * blk, blk), blk)
                for h in heads:
                    kmean_ref[h, pl.ds(n, 1), :] = (jnp.sum(k_ref[rows, col(h)].astype(F32), axis=0, keepdims=True)
                                                    * (1.0 / blk))
                    vt_ref[h, g, :hd, u * blk:(u + 1) * blk] = v_ref[rows, col(h)].T
            for h in heads:
                vt_ref[h, g, hd:, :] = jnp.ones((ones_rows, grp * blk), BF16)
            return c

        lax.fori_loop(0, nb // grp, body, 0)

    own = pl.ds(pl.multiple_of(j * blk, blk), blk)
    qs = [q_ref[:, col(h)] for h in heads]
    gates = [_dot_nt(kmean_ref[h].astype(BF16), qs[h]) for h in heads]
    t_own = [_dot_nt(k_ref[own, col(h)], qs[h]) * c_exp for h in heads]

    row = lax.broadcasted_iota(jnp.int32, gates[0].shape, 0)
    for h in heads:
        g = jnp.where(row < j, gates[h], -jnp.inf)
        maskneg = jnp.full(g.shape, MASK_NEG, F32)
        for r in range(MOBA_TOPK):
            m = jnp.max(g, axis=0, keepdims=True)
            idx = jnp.min(jnp.where(g == m, row, nbp), axis=0, keepdims=True)
            idx = jnp.where(r < j, idx, -1)
            hit = row == idx
            maskneg = jnp.where(hit, 0.0, maskneg)
            g = jnp.where(hit, -jnp.inf, g)
        mask_ref[h] = maskneg

    krow = lax.broadcasted_iota(jnp.int32, (blk, blk), 0)
    qcol = lax.broadcasted_iota(jnp.int32, (blk, blk), 1)
    ones = jnp.ones((ones_rows, blk), BF16)
    state = []
    for h in heads:
        t = jnp.where(krow <= qcol, t_own[h], -jnp.inf)
        m0 = jnp.max(t, axis=0, keepdims=True)
        p = jnp.exp2(t - m0)
        v_own = jnp.concatenate([v_ref[own, col(h)].T, ones], axis=0)
        state.append((m0, jnp.dot(v_own, p.astype(BF16), preferred_element_type=F32)))

    def attend(g, carry):
        rows = pl.ds(pl.multiple_of(g * (grp * blk), grp * blk), grp * blk)
        scores = [_dot_nt(k_ref[rows, col(h)], qs[h]) * c_exp for h in heads]
        new = []
        for h in heads:
            m_prev, acc = carry[h]
            ts = [scores[h][u * blk:(u + 1) * blk] + mask_ref[h, pl.ds(g * grp + u, 1), :] for u in range(grp)]
            m_new = m_prev
            for t in ts:
                m_new = jnp.maximum(m_new, jnp.max(t, axis=0, keepdims=True))
            ps = [jnp.exp2(t - m_new).astype(BF16) for t in ts]
            acc = (jnp.exp2(m_prev - m_new) * acc
                   + jnp.dot(vt_ref[h, g], jnp.concatenate(ps, axis=0), preferred_element_type=F32))
            new.append((m_new, acc))
        return tuple(new)

    final = lax.fori_loop(0, (j + grp - 1) // grp, attend, tuple(state))
    for h in heads:
        _, acc = final[h]
        o_ref[:, col(h)] = (acc[:hd] / acc[hd:hd + 1]).T.astype(o_ref.dtype)


def _moba(proj3):
    b, s, _ = proj3.shape
    blk, hd, grp, nh = MOBA_BLOCK, ATT_HEAD_DIM, MOBA_BLOCKS_PER_STEP, MOBA_HEADS_PER_STEP
    nb = s // blk
    assert s % blk == 0 and nb % grp == 0 and ATT_HEADS % nh == 0
    nbp = -(-nb // 16) * 16
    wide = nh * hd
    qb, kb, vb = COL_Q // wide, COL_K // wide, COL_V // wide
    return pl.pallas_call(
        _moba_kernel,
        grid=(b, ATT_HEADS // nh, nb),
        in_specs=[
            pl.BlockSpec((None, blk, wide), lambda bi, h, j: (bi, j, qb + h)),
            pl.BlockSpec((None, s, wide), lambda bi, h, j: (bi, 0, kb + h)),
            pl.BlockSpec((None, s, wide), lambda bi, h, j: (bi, 0, vb + h)),
        ],
        out_specs=pl.BlockSpec((None, blk, wide), lambda bi, h, j: (bi, j, h)),
        out_shape=jax.ShapeDtypeStruct((b, s, ATT_WIDTH), BF16),
        scratch_shapes=[
            pltpu.VMEM((nh, nbp, hd), F32),
            pltpu.VMEM((nh, nb // grp, hd + BF16_ROWS, grp * blk), BF16),
            pltpu.VMEM((nh, nbp, blk), F32),
        ],
        compiler_params=_params(("parallel", "parallel", "arbitrary")),
        name="moba",
    )(proj3, proj3, proj3)


def _tail_kernel(ga_ref, gb_ref, u_ref, vg_ref, yatt_ref, x_ref, lng_ref, lnb_ref, ws_ref, bias_ref,
                 wba_ref, wbs_ref, wout_ref, gffn_ref, h_ref, hn_ref, ysgu_ref):
    tm = u_ref.shape[0]
    ch, gd = SGU_CHUNK, SGU_GROUP_DIM
    gu = _gelu(u_ref[...].astype(F32))
    gv = _gelu(vg_ref[...].astype(F32))
    mu = jnp.mean(gv, axis=-1, keepdims=True)
    xc = gv - mu
    vn = xc * lax.rsqrt(jnp.mean(xc * xc, axis=-1, keepdims=True) + NORM_EPS)
    vn = (vn * lng_ref[...] + lnb_ref[...]).astype(BF16)
    trow = lax.broadcasted_iota(jnp.int32, (ch, ch), 0)
    tcol = lax.broadcasted_iota(jnp.int32, (ch, ch), 1)
    for g in range(SGU_GROUPS):
        w = jnp.where(tcol <= trow, ws_ref[g], 0.0).astype(BF16)
        cols = slice(g * gd, (g + 1) * gd)
        for c in range(tm // ch):
            rows = slice(c * ch, (c + 1) * ch)
            mixed = jnp.dot(w, vn[rows, cols], preferred_element_type=F32) + bias_ref[:, cols]
            ysgu_ref[rows, cols] = (gu[rows, cols] * mixed).astype(BF16)
    a = jnp.dot(yatt_ref[...], wba_ref[...], preferred_element_type=F32)
    bsg = jnp.dot(ysgu_ref[...], wbs_ref[...], preferred_element_type=F32)
    merged = (jax.nn.sigmoid(ga_ref[...].astype(F32)) * a
              + jax.nn.sigmoid(gb_ref[...].astype(F32)) * bsg)
    h = x_ref[...] + jnp.dot(merged.astype(BF16), wout_ref[...], preferred_element_type=F32)
    h_ref[...] = h
    hn_ref[...] = pltpu.bitcast(_rms(h, gffn_ref[...]).astype(BF16), jnp.uint32)


def _const_spec(shape):
    return pl.BlockSpec(shape, lambda *_: (0,) * len(shape), pipeline_mode=pl.Buffered(1))


def _tail(proj, yatt, x, ln_g, ln_b, w_s, bias_full, w_ba, w_bs, w_out, g_ffn, tm=256):
    t, d = x.shape
    wide, narrow = d, SGU_WIDTH
    return pl.pallas_call(
        _tail_kernel,
        grid=(t // tm,),
        in_specs=[
            pl.BlockSpec((tm, wide), lambda i: (i, COL_GATE_A // wide)),
            pl.BlockSpec((tm, wide), lambda i: (i, COL_GATE_B // wide)),
            pl.BlockSpec((tm, narrow), lambda i: (i, COL_U // narrow)),
            pl.BlockSpec((tm, narrow), lambda i: (i, COL_VG // narrow)),
            pl.BlockSpec((tm, ATT_WIDTH), lambda i: (i, 0)),
            pl.BlockSpec((tm, d), lambda i: (i, 0)),
            _const_spec((1, narrow)),
            _const_spec((1, narrow)),
            _const_spec(w_s.shape),
            _const_spec(bias_full.shape),
            _const_spec(w_ba.shape),
            _const_spec(w_bs.shape),
            _const_spec(w_out.shape),
            _const_spec((1, d)),
        ],
        out_specs=[pl.BlockSpec((tm, d), lambda i: (i, 0)), pl.BlockSpec((tm // 2, d), lambda i: (i, 0))],
        out_shape=[jax.ShapeDtypeStruct((t, d), F32), jax.ShapeDtypeStruct((t // 2, d), jnp.uint32)],
        scratch_shapes=[pltpu.VMEM((tm, narrow), BF16)],
        compiler_params=_params(("parallel",)),
        name="mixer_tail",
    )(proj, proj, proj, proj, yatt, x, ln_g, ln_b, w_s, bias_full, w_ba, w_bs, w_out, g_ffn)


N_EXTRACT = PEER_TOPK + 1
CAND_PAIRS = tuple((a, b) for a in range(N_EXTRACT) for b in range(N_EXTRACT) if (a + 1) * (b + 1) <= N_EXTRACT)
CAND_ROWS = SUBLANES * (1 << (-(-len(CAND_PAIRS) // SUBLANES) - 1).bit_length())
ROUTE_HEADS_PER_STEP = 2


def _extract_top(works, n_iter, n_pad):
    del n_pad
    rows = works[0].shape[0]
    row = lax.broadcasted_iota(jnp.int32, works[0].shape, 0)
    works = list(works)
    vals = [[] for _ in works]
    for _ in range(n_iter):
        for k, work in enumerate(works):
            m = jnp.max(work, axis=0, keepdims=True)
            idx = jnp.min(jnp.where(work == m, row, rows), axis=0, keepdims=True)
            works[k] = jnp.where(row == idx, -jnp.inf, work)
            vals[k].append(m)
    return [(v, jnp.zeros_like(v[0])) for v in vals]


def _sorting_network(n):
    pairs = []
    p = 1
    while p < n:
        k = p
        while k >= 1:
            for j in range(k % p, n - k, 2 * k):
                for i in range(min(k, n - j - k)):
                    if (i + j) // (2 * p) == (i + j + k) // (2 * p):
                        pairs.append((i + j, i + j + k))
            k //= 2
        p *= 2
    return pairs


def _extract_top_distinct(works, n_iter, n_pad):
    del n_pad
    lists = []
    for work in works:
        slabs = [work[SUBLANES * k:SUBLANES * (k + 1)] for k in range(work.shape[0] // SUBLANES)]
        for i, j in _sorting_network(len(slabs)):
            slabs[i], slabs[j] = jnp.maximum(slabs[i], slabs[j]), jnp.minimum(slabs[i], slabs[j])
        lists.append(slabs)
    vals = [[] for _ in works]
    popped = [jnp.zeros_like(slabs[0]) for slabs in lists]
    for r in range(n_iter):
        for k, slabs in enumerate(lists):
            m = jnp.max(slabs[0], axis=0, keepdims=True)
            hit = slabs[0] == m
            for d in range(min(len(slabs), n_iter - r - 1)):
                below = slabs[d + 1] if d + 1 < len(slabs) else -jnp.inf
                slabs[d] = jnp.where(hit, below, slabs[d])
            popped[k] = popped[k] + jnp.where(hit, 1.0, 0.0)
            vals[k].append(m)
    return [(v, jnp.abs(jnp.sum(p, axis=0, keepdims=True) - n_iter)) for v, p in zip(vals, popped)]


def _route_kernel(hn_ref, wqt_ref, keys_ref, tq_ref, e0_ref, s1_ref, e1_ref, cand_ref):
    hn = _unpack(hn_ref[...])
    tm = hn.shape[0]
    group = cand_ref.shape[0]
    cand_ref[...] = jnp.full(cand_ref.shape, -jnp.inf, F32)

    def stats(scores, extract):
        tops = extract([s for pair in scores for s in pair], N_EXTRACT, 0)
        for k in range(group):
            tv0, tv1 = tops[2 * k][0], tops[2 * k + 1][0]
            for r, (a, b) in enumerate(CAND_PAIRS):
                cand_ref[k, r:r + 1, :] = tv0[a] + tv1[b]
        ctops = extract([cand_ref[k] for k in range(group)], N_EXTRACT, CAND_ROWS - len(CAND_PAIRS))
        out = []
        for k, (s0, _) in enumerate(scores):
            c, badc = ctops[k]
            z = jnp.ones_like(c[0])
            for r in range(1, PEER_TOPK):
                z = z + jnp.exp(c[r] - c[0])
            tau = 0.5 * (c[PEER_TOPK - 1] + c[PEER_TOPK])
            bad = tops[2 * k][1] + tops[2 * k + 1][1] + badc
            out.append((tau - s0, jnp.exp(s0 - tops[2 * k][0][0]) / z, tops[2 * k + 1][0][0], bad))
        return out

    def store(ref, hh, val):
        for cc in range(tm // LANES):
            ref[cc, hh] = val[:, cc * LANES:(cc + 1) * LANES]

    def heads(g, carry):
        scores = []
        for k in range(group):
            hh = g * group + k
            w = wqt_ref[pl.ds(pl.multiple_of(hh * 2 * PEER_HALF, 2 * PEER_HALF), 2 * PEER_HALF), :]
            qt = _dot_nt(w, hn).astype(BF16)
            scores.append((jnp.dot(keys_ref[2 * hh], qt[:PEER_HALF], preferred_element_type=F32),
                           jnp.dot(keys_ref[2 * hh + 1], qt[PEER_HALF:], preferred_element_type=F32)))
        fast = stats(scores, _extract_top_distinct)
        bad = None
        for k, (tq, e0, max1, bad_k) in enumerate(fast):
            hh = g * group + k
            store(tq_ref, hh, tq)
            store(e0_ref, hh, e0)
            store(s1_ref, hh, scores[k][1])
            store(e1_ref, hh, jnp.exp(scores[k][1] - max1))
            bad = bad_k if bad is None else bad + bad_k

        @pl.when(jnp.max(bad) > 0.0)
        def _():
            for k, (tq, e0, _, _) in enumerate(stats(scores, _extract_top)):
                store(tq_ref, g * group + k, tq)
                store(e0_ref, g * group + k, e0)

        return carry

    lax.fori_loop(0, PEER_HEADS // group, heads, 0)


def _route(hn, wq_t, keys, tm=512):
    t, d = 2 * hn.shape[0], hn.shape[1]
    nck = tm // LANES
    out_block = pl.BlockSpec((nck, PEER_HEADS, PEER_N_KEYS, LANES), lambda i: (i, 0, 0, 0))
    out_shape = jax.ShapeDtypeStruct((t // LANES, PEER_HEADS, PEER_N_KEYS, LANES), F32)
    return pl.pallas_call(
        _route_kernel,
        grid=(t // tm,),
        in_specs=[
            pl.BlockSpec((tm // 2, d), lambda i: (i, 0)),
            _const_spec(wq_t.shape),
            _const_spec(keys.shape),
        ],
        out_specs=[out_block] * 4,
        out_shape=[out_shape] * 4,
        scratch_shapes=[pltpu.VMEM((ROUTE_HEADS_PER_STEP, CAND_ROWS, tm), F32)],
        compiler_params=_params(("parallel",)),
        name="peer_route",
    )(hn, wq_t, keys)


def _experts_kernel(hn_ref, down_ref, upt_ref, tq_ref, e0_ref, s1_ref, e1_ref, o_ref,
                    acc_ref, hid0_ref, hid1_ref, gh0_ref, gh1_ref, *, n_tiles):
    s = pl.program_id(0)
    te = 2 * down_ref.shape[0]
    tm = 2 * hn_ref.shape[0]

    @pl.when(s == 0)
    def _():
        for ref in (hid0_ref, hid1_ref, gh0_ref, gh1_ref):
            ref[...] = jnp.zeros_like(ref)

    @pl.when((s == 0) | (s % n_tiles == 2 % n_tiles))
    def _():
        acc_ref[...] = jnp.zeros_like(acc_ref)

    def gated_tile(hid_r, gh_w, ii, cc, part):
        n_sub = PEER_N_KEYS // GATE_TILE_PARTS
        sub = slice(part * n_sub, (part + 1) * n_sub)
        rows = slice(ii * PEER_N_KEYS + part * n_sub, ii * PEER_N_KEYS + (part + 1) * n_sub)
        lanes = slice(cc * LANES, (cc + 1) * LANES)
        gate = jnp.zeros((n_sub, LANES), F32)
        for hh in range(PEER_HEADS):
            thr = tq_ref[cc, hh, ii:ii + 1, :]
            scale = e0_ref[cc, hh, ii:ii + 1, :]
            gate = gate + jnp.where(s1_ref[cc, hh, sub, :] >= thr, e1_ref[cc, hh, sub, :], 0.0) * scale
        gh_w[rows, lanes] = (gate * _gelu(hid_r[rows, lanes])).astype(BF16)

    def step(hid_w, hid_r, gh_w, gh_r):
        tiles = [(ii, cc) for ii in range(te // PEER_N_KEYS) for cc in range(tm // LANES)]
        n_m = EXPERT_M_PIECES
        n_k = len(tiles) // (2 * n_m)
        d = acc_ref.shape[0]
        tile = iter(tiles)
        for nh in range(2):
            lanes = slice(nh * (tm // 2), (nh + 1) * (tm // 2))
            words_t = slice(nh * (tm // 4), (nh + 1) * (tm // 4))
            for mh in range(n_m):
                rows_c = slice(mh * (d // n_m), (mh + 1) * (d // n_m))
                words_c = slice(mh * (d // n_m // 2), (mh + 1) * (d // n_m // 2))
                rows_a = slice(mh * (te // n_m), (mh + 1) * (te // n_m))
                words_a = slice(mh * (te // n_m // 2), (mh + 1) * (te // n_m // 2))
                up = down = None
                for kc in range(n_k):
                    ii, cc = next(tile)
                    kk_a = slice(kc * (d // n_k), (kc + 1) * (d // n_k))
                    part = _dot_nt(_unpack(down_ref[words_a, kk_a]), _unpack(hn_ref[words_t, kk_a]))
                    down = part if down is None else down + part
                    gated_tile(hid_r, gh_w, ii, cc, 0)
                    kk_c = slice(kc * (te // n_k), (kc + 1) * (te // n_k))
                    part = jnp.dot(_unpack(upt_ref[words_c, kk_c]), gh_r[kk_c, lanes],
                                   preferred_element_type=F32)
                    up = part if up is None else up + part
                    gated_tile(hid_r, gh_w, ii, cc, 1)
                hid_w[rows_a, lanes] = down
                acc_ref[rows_c, lanes] += up

    @pl.when(s % 2 == 0)
    def _():
        step(hid0_ref, hid1_ref, gh1_ref, gh0_ref)

    @pl.when(s % 2 == 1)
    def _():
        step(hid1_ref, hid0_ref, gh0_ref, gh1_ref)

    @pl.when((s >= 2) & ((s - 2) % n_tiles == n_tiles - 1))
    def _():
        o_ref[...] = acc_ref[...].T


def _experts(hn, down, up_t, tq, e0, s1, e1, tm=512, te=1024):
    t, d = 2 * hn.shape[0], hn.shape[1]
    nt, ne = t // tm, 2 * down.shape[0] // te
    nck = tm // LANES
    ni = te // PEER_N_KEYS
    assert ni % 8 == 0
    tok_a = lambda s: jnp.minimum(s // ne, nt - 1)
    tok_b = lambda s: jnp.minimum(jnp.maximum(s - 1, 0) // ne, nt - 1)
    tok_c = lambda s: jnp.minimum(jnp.maximum(s - 2, 0) // ne, nt - 1)
    exp_b = lambda s: jnp.maximum(s - 1, 0) % ne
    exp_c = lambda s: jnp.maximum(s - 2, 0) % ne
    row_block = pl.BlockSpec((nck, PEER_HEADS, ni, LANES), lambda s: (tok_b(s), 0, exp_b(s), 0))
    full_block = pl.BlockSpec((nck, PEER_HEADS, PEER_N_KEYS, LANES), lambda s: (tok_b(s), 0, 0, 0))
    return pl.pallas_call(
        functools.partial(_experts_kernel, n_tiles=ne),
        grid=(nt * ne + 2,),
        in_specs=[
            pl.BlockSpec((tm // 2, d), lambda s: (tok_a(s), 0)),
            pl.BlockSpec((te // 2, d), lambda s: (s % ne, 0)),
            pl.BlockSpec((d // 2, te), lambda s: (0, exp_c(s))),
            row_block, row_block, full_block, full_block,
        ],
        out_specs=pl.BlockSpec((tm, d), lambda s: (tok_c(s), 0)),
        out_shape=jax.ShapeDtypeStruct((t, d), F32),
        scratch_shapes=[
            pltpu.VMEM((d, tm), F32),
            pltpu.VMEM((te, tm), F32), pltpu.VMEM((te, tm), F32),
            pltpu.VMEM((te, tm), BF16), pltpu.VMEM((te, tm), BF16),
        ],
        compiler_params=_params(("arbitrary",)),
        name="peer_experts",
    )(hn, down, up_t, tq, e0, s1, e1)


def _ple_kernel(h_ref, y_ref, p_ref, gple_ref, wgate_ref, wproj_ref, gfin_ref, o_ref, *, final):
    h = h_ref[...] + y_ref[...]
    gate = jax.nn.sigmoid(jnp.dot(_rms(h, gple_ref[...]).astype(BF16), wgate_ref[...], preferred_element_type=F32))
    emb = jnp.dot(p_ref[...].astype(BF16), wproj_ref[...], preferred_element_type=F32)
    h = h + gate * emb
    o_ref[...] = _rms(h, gfin_ref[...]) if final else h


def _ple(h, y, p, g_ple, w_gate, w_proj, g_final, final, tm=512):
    t, d = h.shape
    pd = p.shape[1]
    return pl.pallas_call(
        functools.partial(_ple_kernel, final=final),
        grid=(t // tm,),
        in_specs=[
            pl.BlockSpec((tm, d), lambda i: (i, 0)),
            pl.BlockSpec((tm, d), lambda i: (i, 0)),
            pl.BlockSpec((tm, pd), lambda i: (i, 0)),
            _const_spec((1, d)),
            _const_spec(w_gate.shape),
            _const_spec(w_proj.shape),
            _const_spec((1, d)),
        ],
        out_specs=pl.BlockSpec((tm, d), lambda i: (i, 0)),
        out_shape=jax.ShapeDtypeStruct((t, d), F32),
        compiler_params=_params(("parallel",)),
        name="ple",
    )(h, y, p, g_ple, w_gate, w_proj, g_final)


def kernel(x, p, norm_mix_g, w_in, sgu_ln_g, sgu_ln_b, sgu_w, sgu_b, w_branch_attn, w_branch_sgu, w_out, norm_ffn_g, peer_w_query, peer_sub_keys, peer_down, peer_up, norm_ple_g, ple_w_proj, ple_w_gate, final_norm_g):
    b, s, d = x.shape
    t = b * s
    depth = w_in.shape[0]
    h = x.reshape(t, d)
    row = lambda v: v.reshape(1, -1)
    aw = ATT_WIDTH
    for i in range(depth):
        proj = _inproj(h, row(norm_mix_g[i]), w_in[i].astype(BF16), col_shift=3 * aw + 2 * SGU_WIDTH)
        yatt = _moba(proj.reshape(b, s, -1)).reshape(t, aw)
        bias_full = jnp.repeat(sgu_b[i].T, SGU_GROUP_DIM, axis=1)
        h, hn = _tail(proj, yatt, h, row(sgu_ln_g[i]), row(sgu_ln_b[i]), sgu_w[i], bias_full,
                      w_branch_attn[i].astype(BF16), w_branch_sgu[i].astype(BF16), w_out[i].astype(BF16),
                      row(norm_ffn_g[i]))
        keys = peer_sub_keys[i].reshape(2 * PEER_HEADS, PEER_N_KEYS, PEER_HALF).astype(BF16)
        tq, e0, s1, e1 = _route(hn, peer_w_query[i].T.astype(BF16), keys)
        y = _experts(hn, _pack_bf16(peer_down[i]), _pack_bf16(peer_up[i], transpose=True), tq, e0, s1, e1)
        h = _ple(h, y, p[i].reshape(t, -1), row(norm_ple_g[i]), ple_w_gate[i].astype(BF16),
                 ple_w_proj[i].astype(BF16), row(final_norm_g), final=(i == depth - 1))
    if depth == 0:
        raise ValueError("depth must be at least 1")
    return h.reshape(b, s, d)
```

```python
import functools

import jax
import jax.numpy as jnp
from jax import lax
from jax.experimental import pallas as pl
from jax.experimental.pallas import tpu as pltpu

F32 = jnp.float32
BF16 = jnp.bfloat16

NORM_EPS = 1e-6
ATT_HEADS = 8
ATT_HEAD_DIM = 128
ATT_WIDTH = ATT_HEADS * ATT_HEAD_DIM
MOBA_BLOCK = 256
MOBA_TOPK = 3
MOBA_BLOCKS_PER_STEP = 4
MOBA_HEADS_PER_STEP = 4
SGU_GROUPS = 8
SGU_GROUP_DIM = 128
SGU_WIDTH = SGU_GROUPS * SGU_GROUP_DIM
SGU_CHUNK = 128
PEER_HEADS = 8
PEER_N_KEYS = 128
PEER_HALF = 128
PEER_TOPK = 16

GATE_TILE_PARTS = 2
EXPERT_M_PIECES = 4
LANES = 128
SUBLANES = 8
BF16_ROWS = 2 * SUBLANES
VMEM_LIMIT = 56 * 1024 * 1024
MASK_NEG = -(2.0 ** 30)
LOG2E = 1.4426950408889634

D_MODEL = 2048
COL_GATE_A = 0
COL_GATE_B = D_MODEL
COL_Q = 2 * D_MODEL
COL_K = COL_Q + ATT_WIDTH
COL_V = COL_K + ATT_WIDTH
COL_U = COL_V + ATT_WIDTH
COL_VG = COL_U + SGU_WIDTH


def _params(semantics):
    return pltpu.CompilerParams(dimension_semantics=semantics, vmem_limit_bytes=VMEM_LIMIT)


def _rms(x, gain):
    return x * lax.rsqrt(jnp.mean(x * x, axis=-1, keepdims=True) + NORM_EPS) * gain


def _gelu(x):
    return 0.5 * x * (1.0 + lax.erf(x * (0.5 ** 0.5)))


def _dot_nt(a, b):
    return lax.dot_general(a, b, (((1,), (1,)), ((), ())), preferred_element_type=F32)


def _unpack(words):
    return pltpu.bitcast(words, BF16)


def _pack_kernel(x_ref, o_ref, *, transpose):
    x = x_ref[...]
    if transpose:
        x = x.T
    o_ref[...] = pltpu.bitcast(x.astype(BF16), jnp.uint32)


def _pack_bf16(x, transpose=False, tr=512):
    r, c = x.shape
    if transpose:
        out_shape, out_block, out_map = (c // 2, r), (c // 2, tr), lambda i: (0, i)
    else:
        out_shape, out_block, out_map = (r // 2, c), (tr // 2, c), lambda i: (i, 0)
    return pl.pallas_call(
        functools.partial(_pack_kernel, transpose=transpose),
        grid=(r // tr,),
        in_specs=[pl.BlockSpec((tr, c), lambda i: (i, 0))],
        out_specs=pl.BlockSpec(out_block, out_map),
        out_shape=jax.ShapeDtypeStruct(out_shape, jnp.uint32),
        compiler_params=_params(("parallel",)),
        name="pack_bf16_t" if transpose else "pack_bf16",
    )(x)


def _inproj_kernel(x_ref, g_ref, w_ref, o_ref, xn_ref):
    @pl.when(pl.program_id(1) == 0)
    def _():
        xn_ref[...] = _rms(x_ref[...], g_ref[...]).astype(BF16)

    o_ref[...] = jnp.dot(xn_ref[...], w_ref[...], preferred_element_type=F32).astype(o_ref.dtype)


def _inproj(x, gain, w, col_shift, tm=1024, tn=512):
    t, d = x.shape
    n = w.shape[1]
    assert col_shift % tn == 0
    shift = col_shift // tn
    return pl.pallas_call(
        _inproj_kernel,
        grid=(t // tm, n // tn),
        in_specs=[
            pl.BlockSpec((tm, d), lambda i, j: (i, 0)),
            pl.BlockSpec((1, d), lambda i, j: (0, 0)),
            pl.BlockSpec((d, tn), lambda i, j: (0, (j + shift) % (n // tn))),
        ],
        out_specs=pl.BlockSpec((tm, tn), lambda i, j: (i, j)),
        out_shape=jax.ShapeDtypeStruct((t, n), BF16),
        scratch_shapes=[pltpu.VMEM((tm, d), BF16)],
        compiler_params=_params(("parallel", "arbitrary")),
        name="inproj",
    )(x, gain, w)


def _moba_kernel(q_ref, k_ref, v_ref, o_ref, kmean_ref, vt_ref, mask_ref, score_ref):
    j = pl.program_id(2)
    blk, hd, grp = MOBA_BLOCK, ATT_HEAD_DIM, MOBA_BLOCKS_PER_STEP
    heads = range(q_ref.shape[1] // hd)
    nb = k_ref.shape[0] // blk
    nbp = kmean_ref.shape[1]
    ones_rows = vt_ref.shape[2] - hd
    c_exp = (hd ** -0.5) * LOG2E

    def col(h):
        return slice(h * hd, (h + 1) * hd)

    @pl.when(j == 0)
    def _():
        kmean_ref[...] = jnp.zeros_like(kmean_ref)

        def body(g, c):
            for u in range(grp):
                n = g * grp + u
                rows = pl.ds(pl.multiple_of(n * blk, blk), blk)
                for h in heads:
                    kmean_ref[h, pl.ds(n, 1), :] = (jnp.sum(k_ref[rows, col(h)].astype(F32), axis=0, keepdims=True)
                                                    * (1.0 / blk))
                    vt_ref[h, g, :hd, u * blk:(u + 1) * blk] = v_ref[rows, col(h)].T
            for h in heads:
                vt_ref[h, g, hd:, :] = jnp.ones((ones_rows, grp * blk), BF16)
            return c

        lax.fori_loop(0, nb // grp, body, 0)

    own = pl.ds(pl.multiple_of(j * blk, blk), blk)
    qs = [q_ref[:, col(h)] for h in heads]
    gates = [_dot_nt(kmean_ref[h].astype(BF16), qs[h]) for h in heads]
    t_own = [_dot_nt(k_ref[own, col(h)], qs[h]) * c_exp for h in heads]

    def group_scores(h, g):
        rows = pl.ds(pl.multiple_of(g * (grp * blk), grp * blk), grp * blk)
        return _dot_nt(k_ref[rows, col(h)], qs[h]) * c_exp

    row = lax.broadcasted_iota(jnp.int32, gates[0].shape, 0)
    for h in heads:
        g = jnp.where(row < j, gates[h], -jnp.inf)
        maskneg = jnp.full(g.shape, MASK_NEG, F32)
        for r in range(MOBA_TOPK):
            m = jnp.max(g, axis=0, keepdims=True)
            idx = jnp.min(jnp.where(g == m, row, nbp), axis=0, keepdims=True)
            idx = jnp.where(r < j, idx, -1)
            hit = row == idx
            maskneg = jnp.where(hit, 0.0, maskneg)
            g = jnp.where(hit, -jnp.inf, g)
        mask_ref[h] = maskneg
        score_ref[h] = group_scores(h, 0)

    krow = lax.broadcasted_iota(jnp.int32, (blk, blk), 0)
    qcol = lax.broadcasted_iota(jnp.int32, (blk, blk), 1)
    ones = jnp.ones((ones_rows, blk), BF16)
    state = []
    for h in heads:
        t = jnp.where(krow <= qcol, t_own[h], -jnp.inf)
        m0 = jnp.max(t, axis=0, keepdims=True)
        p = jnp.exp2(t - m0)
        v_own = jnp.concatenate([v_ref[own, col(h)].T, ones], axis=0)
        state.append((m0, jnp.dot(v_own, p.astype(BF16), preferred_element_type=F32)))

    def attend(g, carry):
        g_next = jnp.minimum(g + 1, nb // grp - 1)
        new = []
        for h in heads:
            m_prev, acc = carry[h]
            scores = score_ref[h]
            ts = [scores[u * blk:(u + 1) * blk] + mask_ref[h, pl.ds(g * grp + u, 1), :] for u in range(grp)]
            m_new = m_prev
            for t in ts:
                m_new = jnp.maximum(m_new, jnp.max(t, axis=0, keepdims=True))
            ps = [jnp.exp2(t - m_new).astype(BF16) for t in ts]
            score_ref[h] = group_scores(h, g_next)
            acc = (jnp.exp2(m_prev - m_new) * acc
                   + jnp.dot(vt_ref[h, g], jnp.concatenate(ps, axis=0), preferred_element_type=F32))
            new.append((m_new, acc))
        return tuple(new)

    final = lax.fori_loop(0, (j + grp - 1) // grp, attend, tuple(state))
    for h in heads:
        _, acc = final[h]
        o_ref[:, col(h)] = (acc[:hd] / acc[hd:hd + 1]).T.astype(o_ref.dtype)


def _moba(proj3):
    b, s, _ = proj3.shape
    blk, hd, grp, nh = MOBA_BLOCK, ATT_HEAD_DIM, MOBA_BLOCKS_PER_STEP, MOBA_HEADS_PER_STEP
    nb = s // blk
    assert s % blk == 0 and nb % grp == 0 and ATT_HEADS % nh == 0
    nbp = -(-nb // 16) * 16
    wide = nh * hd
    qb, kb, vb = COL_Q // wide, COL_K // wide, COL_V // wide
    return pl.pallas_call(
        _moba_kernel,
        grid=(b, ATT_HEADS // nh, nb),
        in_specs=[
            pl.BlockSpec((None, blk, wide), lambda bi, h, j: (bi, j, qb + h)),
            pl.BlockSpec((None, s, wide), lambda bi, h, j: (bi, 0, kb + h)),
            pl.BlockSpec((None, s, wide), lambda bi, h, j: (bi, 0, vb + h)),
        ],
        out_specs=pl.BlockSpec((None, blk, wide), lambda bi, h, j: (bi, j, h)),
        out_shape=jax.ShapeDtypeStruct((b, s, ATT_WIDTH), BF16),
        scratch_shapes=[
            pltpu.VMEM((nh, nbp, hd), F32),
            pltpu.VMEM((nh, nb // grp, hd + BF16_ROWS, grp * blk), BF16),
            pltpu.VMEM((nh, nbp, blk), F32),
            pltpu.VMEM((nh, grp * blk, blk), F32),
        ],
        compiler_params=_params(("parallel", "parallel", "arbitrary")),
        name="moba",
    )(proj3, proj3, proj3)


def _tail_kernel(ga_ref, gb_ref, u_ref, vg_ref, yatt_ref, x_ref, lng_ref, lnb_ref, ws_ref, bias_ref,
                 wba_ref, wbs_ref, wout_ref, gffn_ref, h_ref, hn_ref, ysgu_ref):
    tm = u_ref.shape[0]
    ch, gd = SGU_CHUNK, SGU_GROUP_DIM
    gu = _gelu(u_ref[...].astype(F32))
    gv = _gelu(vg_ref[...].astype(F32))
    mu = jnp.mean(gv, axis=-1, keepdims=True)
    xc = gv - mu
    vn = xc * lax.rsqrt(jnp.mean(xc * xc, axis=-1, keepdims=True) + NORM_EPS)
    vn = (vn * lng_ref[...] + lnb_ref[...]).astype(BF16)
    trow = lax.broadcasted_iota(jnp.int32, (ch, ch), 0)
    tcol = lax.broadcasted_iota(jnp.int32, (ch, ch), 1)
    for g in range(SGU_GROUPS):
        w = jnp.where(tcol <= trow, ws_ref[g], 0.0).astype(BF16)
        cols = slice(g * gd, (g + 1) * gd)
        for c in range(tm // ch):
            rows = slice(c * ch, (c + 1) * ch)
            mixed = jnp.dot(w, vn[rows, cols], preferred_element_type=F32) + bias_ref[:, cols]
            ysgu_ref[rows, cols] = (gu[rows, cols] * mixed).astype(BF16)
    a = jnp.dot(yatt_ref[...], wba_ref[...], preferred_element_type=F32)
    bsg = jnp.dot(ysgu_ref[...], wbs_ref[...], preferred_element_type=F32)
    merged = (jax.nn.sigmoid(ga_ref[...].astype(F32)) * a
              + jax.nn.sigmoid(gb_ref[...].astype(F32)) * bsg)
    h = x_ref[...] + jnp.dot(merged.astype(BF16), wout_ref[...], preferred_element_type=F32)
    h_ref[...] = h
    hn_ref[...] = pltpu.bitcast(_rms(h, gffn_ref[...]).astype(BF16), jnp.uint32)


def _const_spec(shape):
    return pl.BlockSpec(shape, lambda *_: (0,) * len(shape), pipeline_mode=pl.Buffered(1))


def _tail(proj, yatt, x, ln_g, ln_b, w_s, bias_full, w_ba, w_bs, w_out, g_ffn, tm=256):
    t, d = x.shape
    wide, narrow = d, SGU_WIDTH
    return pl.pallas_call(
        _tail_kernel,
        grid=(t // tm,),
        in_specs=[
            pl.BlockSpec((tm, wide), lambda i: (i, COL_GATE_A // wide)),
            pl.BlockSpec((tm, wide), lambda i: (i, COL_GATE_B // wide)),
            pl.BlockSpec((tm, narrow), lambda i: (i, COL_U // narrow)),
            pl.BlockSpec((tm, narrow), lambda i: (i, COL_VG // narrow)),
            pl.BlockSpec((tm, ATT_WIDTH), lambda i: (i, 0)),
            pl.BlockSpec((tm, d), lambda i: (i, 0)),
            _const_spec((1, narrow)),
            _const_spec((1, narrow)),
            _const_spec(w_s.shape),
            _const_spec(bias_full.shape),
            _const_spec(w_ba.shape),
            _const_spec(w_bs.shape),
            _const_spec(w_out.shape),
            _const_spec((1, d)),
        ],
        out_specs=[pl.BlockSpec((tm, d), lambda i: (i, 0)), pl.BlockSpec((tm // 2, d), lambda i: (i, 0))],
        out_shape=[jax.ShapeDtypeStruct((t, d), F32), jax.ShapeDtypeStruct((t // 2, d), jnp.uint32)],
        scratch_shapes=[pltpu.VMEM((tm, narrow), BF16)],
        compiler_params=_params(("parallel",)),
        name="mixer_tail",
    )(proj, proj, proj, proj, yatt, x, ln_g, ln_b, w_s, bias_full, w_ba, w_bs, w_out, g_ffn)


N_EXTRACT = PEER_TOPK + 1
CAND_PAIRS = tuple((a, b) for a in range(N_EXTRACT) for b in range(N_EXTRACT) if (a + 1) * (b + 1) <= N_EXTRACT)
CAND_ROWS = SUBLANES * (1 << (-(-len(CAND_PAIRS) // SUBLANES) - 1).bit_length())
ROUTE_HEADS_PER_STEP = 2


def _extract_top(works, n_iter, n_pad):
    del n_pad
    rows = works[0].shape[0]
    row = lax.broadcasted_iota(jnp.int32, works[0].shape, 0)
    works = list(works)
    vals = [[] for _ in works]
    for _ in range(n_iter):
        for k, work in enumerate(works):
            m = jnp.max(work, axis=0, keepdims=True)
            idx = jnp.min(jnp.where(work == m, row, rows), axis=0, keepdims=True)
            works[k] = jnp.where(row == idx, -jnp.inf, work)
            vals[k].append(m)
    return [(v, jnp.zeros_like(v[0])) for v in vals]


def _sorting_network(n):
    pairs = []
    p = 1
    while p < n:
        k = p
        while k >= 1:
            for j in range(k % p, n - k, 2 * k):
                for i in range(min(k, n - j - k)):
                    if (i + j) // (2 * p) == (i + j + k) // (2 * p):
                        pairs.append((i + j, i + j + k))
            k //= 2
        p *= 2
    return pairs


def _extract_top_distinct(works, n_iter, n_pad):
    del n_pad
    lists = []
    for work in works:
        slabs = [work[SUBLANES * k:SUBLANES * (k + 1)] for k in range(work.shape[0] // SUBLANES)]
        for i, j in _sorting_network(len(slabs)):
            slabs[i], slabs[j] = jnp.maximum(slabs[i], slabs[j]), jnp.minimum(slabs[i], slabs[j])
        lists.append(slabs)
    vals = [[] for _ in works]
    popped = [jnp.zeros_like(slabs[0]) for slabs in lists]
    for r in range(n_iter):
        for k, slabs in enumerate(lists):
            m = jnp.max(slabs[0], axis=0, keepdims=True)
            hit = slabs[0] == m
            for d in range(min(len(slabs), n_iter - r - 1)):
                below = slabs[d + 1] if d + 1 < len(slabs) else -jnp.inf
                slabs[d] = jnp.where(hit, below, slabs[d])
            popped[k] = popped[k] + jnp.where(hit, 1.0, 0.0)
            vals[k].append(m)
    return [(v, jnp.abs(jnp.sum(p, axis=0, keepdims=True) - n_iter)) for v, p in zip(vals, popped)]


def _route_kernel(hn_ref, wqt_ref, keys_ref, tq_ref, e0_ref, s1_ref, e1_ref, cand_ref):
    hn = _unpack(hn_ref[...])
    tm = hn.shape[0]
    group = cand_ref.shape[0]
    cand_ref[...] = jnp.full(cand_ref.shape, -jnp.inf, F32)

    def stats(scores, extract):
        tops = extract([s for pair in scores for s in pair], N_EXTRACT, 0)
        for k in range(group):
            tv0, tv1 = tops[2 * k][0], tops[2 * k + 1][0]
            for r, (a, b) in enumerate(CAND_PAIRS):
                cand_ref[k, r:r + 1, :] = tv0[a] + tv1[b]
        ctops = extract([cand_ref[k] for k in range(group)], N_EXTRACT, CAND_ROWS - len(CAND_PAIRS))
        out = []
        for k, (s0, _) in enumerate(scores):
            c, badc = ctops[k]
            z = jnp.ones_like(c[0])
            for r in range(1, PEER_TOPK):
                z = z + jnp.exp(c[r] - c[0])
            tau = 0.5 * (c[PEER_TOPK - 1] + c[PEER_TOPK])
            bad = tops[2 * k][1] + tops[2 * k + 1][1] + badc
            out.append((tau - s0, jnp.exp(s0 - tops[2 * k][0][0]) / z, tops[2 * k + 1][0][0], bad))
        return out

    def store(ref, hh, val):
        for cc in range(tm // LANES):
            ref[cc, hh] = val[:, cc * LANES:(cc + 1) * LANES]

    def heads(g, carry):
        scores = []
        for k in range(group):
            hh = g * group + k
            w = wqt_ref[pl.ds(pl.multiple_of(hh * 2 * PEER_HALF, 2 * PEER_HALF), 2 * PEER_HALF), :]
            qt = _dot_nt(w, hn).astype(BF16)
            scores.append((jnp.dot(keys_ref[2 * hh], qt[:PEER_HALF], preferred_element_type=F32),
                           jnp.dot(keys_ref[2 * hh + 1], qt[PEER_HALF:], preferred_element_type=F32)))
        fast = stats(scores, _extract_top_distinct)
        bad = None
        for k, (tq, e0, max1, bad_k) in enumerate(fast):
            hh = g * group + k
            store(tq_ref, hh, tq)
            store(e0_ref, hh, e0)
            store(s1_ref, hh, scores[k][1])
            store(e1_ref, hh, jnp.exp(scores[k][1] - max1))
            bad = bad_k if bad is None else bad + bad_k

        @pl.when(jnp.max(bad) > 0.0)
        def _():
            for k, (tq, e0, _, _) in enumerate(stats(scores, _extract_top)):
                store(tq_ref, g * group + k, tq)
                store(e0_ref, g * group + k, e0)

        return carry

    lax.fori_loop(0, PEER_HEADS // group, heads, 0)


def _route(hn, wq_t, keys, tm=512):
    t, d = 2 * hn.shape[0], hn.shape[1]
    nck = tm // LANES
    out_block = pl.BlockSpec((nck, PEER_HEADS, PEER_N_KEYS, LANES), lambda i: (i, 0, 0, 0))
    out_shape = jax.ShapeDtypeStruct((t // LANES, PEER_HEADS, PEER_N_KEYS, LANES), F32)
    return pl.pallas_call(
        _route_kernel,
        grid=(t // tm,),
        in_specs=[
            pl.BlockSpec((tm // 2, d), lambda i: (i, 0)),
            _const_spec(wq_t.shape),
            _const_spec(keys.shape),
        ],
        out_specs=[out_block] * 4,
        out_shape=[out_shape] * 4,
        scratch_shapes=[pltpu.VMEM((ROUTE_HEADS_PER_STEP, CAND_ROWS, tm), F32)],
        compiler_params=_params(("parallel",)),
        name="peer_route",
    )(hn, wq_t, keys)


def _experts_kernel(hn_ref, down_ref, upt_ref, tq_ref, e0_ref, s1_ref, e1_ref, o_ref,
                    acc_ref, hid0_ref, hid1_ref, gh0_ref, gh1_ref, *, n_tiles):
    s = pl.program_id(0)
    te = 2 * down_ref.shape[0]
    tm = 2 * hn_ref.shape[0]

    @pl.when(s == 0)
    def _():
        for ref in (hid0_ref, hid1_ref, gh0_ref, gh1_ref):
            ref[...] = jnp.zeros_like(ref)

    @pl.when((s == 0) | (s % n_tiles == 2 % n_tiles))
    def _():
        acc_ref[...] = jnp.zeros_like(acc_ref)

    def gated_tile(hid_r, gh_w, ii, cc, part):
        n_sub = PEER_N_KEYS // GATE_TILE_PARTS
        sub = slice(part * n_sub, (part + 1) * n_sub)
        rows = slice(ii * PEER_N_KEYS + part * n_sub, ii * PEER_N_KEYS + (part + 1) * n_sub)
        lanes = slice(cc * LANES, (cc + 1) * LANES)
        gate = jnp.zeros((n_sub, LANES), F32)
        for hh in range(PEER_HEADS):
            thr = tq_ref[cc, hh, ii:ii + 1, :]
            scale = e0_ref[cc, hh, ii:ii + 1, :]
            gate = gate + jnp.where(s1_ref[cc, hh, sub, :] >= thr, e1_ref[cc, hh, sub, :], 0.0) * scale
        gh_w[rows, lanes] = (gate * _gelu(hid_r[rows, lanes])).astype(BF16)

    def step(hid_w, hid_r, gh_w, gh_r):
        tiles = [(ii, cc) for ii in range(te // PEER_N_KEYS) for cc in range(tm // LANES)]
        n_m = EXPERT_M_PIECES
        n_k = len(tiles) // (2 * n_m)
        d = acc_ref.shape[0]
        tile = iter(tiles)
        for nh in range(2):
            lanes = slice(nh * (tm // 2), (nh + 1) * (tm // 2))
            words_t = slice(nh * (tm // 4), (nh + 1) * (tm // 4))
            for mh in range(n_m):
                rows_c = slice(mh * (d // n_m), (mh + 1) * (d // n_m))
                words_c = slice(mh * (d // n_m // 2), (mh + 1) * (d // n_m // 2))
                rows_a = slice(mh * (te // n_m), (mh + 1) * (te // n_m))
                words_a = slice(mh * (te // n_m // 2), (mh + 1) * (te // n_m // 2))
                up = down = None
                for kc in range(n_k):
                    ii, cc = next(tile)
                    kk_a = slice(kc * (d // n_k), (kc + 1) * (d // n_k))
                    part = _dot_nt(_unpack(down_ref[words_a, kk_a]), _unpack(hn_ref[words_t, kk_a]))
                    down = part if down is None else down + part
                    gated_tile(hid_r, gh_w, ii, cc, 0)
                    kk_c = slice(kc * (te // n_k), (kc + 1) * (te // n_k))
                    part = jnp.dot(_unpack(upt_ref[words_c, kk_c]), gh_r[kk_c, lanes],
                                   preferred_element_type=F32)
                    up = part if up is None else up + part
                    gated_tile(hid_r, gh_w, ii, cc, 1)
                hid_w[rows_a, lanes] = down
                acc_ref[rows_c, lanes] += up

    @pl.when(s % 2 == 0)
    def _():
        step(hid0_ref, hid1_ref, gh1_ref, gh0_ref)

    @pl.when(s % 2 == 1)
    def _():
        step(hid1_ref, hid0_ref, gh0_ref, gh1_ref)

    @pl.when((s >= 2) & ((s - 2) % n_tiles == n_tiles - 1))
    def _():
        o_ref[...] = acc_ref[...].T


def _experts(hn, down, up_t, tq, e0, s1, e1, tm=512, te=1024):
    t, d = 2 * hn.shape[0], hn.shape[1]
    nt, ne = t // tm, 2 * down.shape[0] // te
    nck = tm // LANES
    ni = te // PEER_N_KEYS
    assert ni % 8 == 0
    tok_a = lambda s: jnp.minimum(s // ne, nt - 1)
    tok_b = lambda s: jnp.minimum(jnp.maximum(s - 1, 0) // ne, nt - 1)
    tok_c = lambda s: jnp.minimum(jnp.maximum(s - 2, 0) // ne, nt - 1)
    exp_b = lambda s: jnp.maximum(s - 1, 0) % ne
    exp_c = lambda s: jnp.maximum(s - 2, 0) % ne
    row_block = pl.BlockSpec((nck, PEER_HEADS, ni, LANES), lambda s: (tok_b(s), 0, exp_b(s), 0))
    full_block = pl.BlockSpec((nck, PEER_HEADS, PEER_N_KEYS, LANES), lambda s: (tok_b(s), 0, 0, 0))
    return pl.pallas_call(
        functools.partial(_experts_kernel, n_tiles=ne),
        grid=(nt * ne + 2,),
        in_specs=[
            pl.BlockSpec((tm // 2, d), lambda s: (tok_a(s), 0)),
            pl.BlockSpec((te // 2, d), lambda s: (s % ne, 0)),
            pl.BlockSpec((d // 2, te), lambda s: (0, exp_c(s))),
            row_block, row_block, full_block, full_block,
        ],
        out_specs=pl.BlockSpec((tm, d), lambda s: (tok_c(s), 0)),
        out_shape=jax.ShapeDtypeStruct((t, d), F32),
        scratch_shapes=[
            pltpu.VMEM((d, tm), F32),
            pltpu.VMEM((te, tm), F32), pltpu.VMEM((te, tm), F32),
            pltpu.VMEM((te, tm), BF16), pltpu.VMEM((te, tm), BF16),
        ],
        compiler_params=_params(("arbitrary",)),
        name="peer_experts",
    )(hn, down, up_t, tq, e0, s1, e1)


def _ple_kernel(h_ref, y_ref, p_ref, gple_ref, wgate_ref, wproj_ref, gfin_ref, o_ref, *, final):
    h = h_ref[...] + y_ref[...]
    gate = jax.nn.sigmoid(jnp.dot(_rms(h, gple_ref[...]).astype(BF16), wgate_ref[...], preferred_element_type=F32))
    emb = jnp.dot(p_ref[...].astype(BF16), wproj_ref[...], preferred_element_type=F32)
    h = h + gate * emb
    o_ref[...] = _rms(h, gfin_ref[...]) if final else h


def _ple(h, y, p, g_ple, w_gate, w_proj, g_final, final, tm=512):
    t, d = h.shape
    pd = p.shape[1]
    return pl.pallas_call(
        functools.partial(_ple_kernel, final=final),
        grid=(t // tm,),
        in_specs=[
            pl.BlockSpec((tm, d), lambda i: (i, 0)),
            pl.BlockSpec((tm, d), lambda i: (i, 0)),
            pl.BlockSpec((tm, pd), lambda i: (i, 0)),
            _const_spec((1, d)),
            _const_spec(w_gate.shape),
            _const_spec(w_proj.shape),
            _const_spec((1, d)),
        ],
        out_specs=pl.BlockSpec((tm, d), lambda i: (i, 0)),
        out_shape=jax.ShapeDtypeStruct((t, d), F32),
        compiler_params=_params(("parallel",)),
        name="ple",
    )(h, y, p, g_ple, w_gate, w_proj, g_final)


def kernel(x, p, norm_mix_g, w_in, sgu_ln_g, sgu_ln_b, sgu_w, sgu_b, w_branch_attn, w_branch_sgu, w_out, norm_ffn_g, peer_w_query, peer_sub_keys, peer_down, peer_up, norm_ple_g, ple_w_proj, ple_w_gate, final_norm_g):
    b, s, d = x.shape
    assert d == D_MODEL
    t = b * s
    depth = w_in.shape[0]
    h = x.reshape(t, d)
    row = lambda v: v.reshape(1, -1)
    aw = ATT_WIDTH
    for i in range(depth):
        proj = _inproj(h, row(norm_mix_g[i]), w_in[i].astype(BF16), col_shift=3 * aw + 2 * SGU_WIDTH)
        yatt = _moba(proj.reshape(b, s, -1)).reshape(t, aw)
        bias_full = jnp.repeat(sgu_b[i].T, SGU_GROUP_DIM, axis=1)
        h, hn = _tail(proj, yatt, h, row(sgu_ln_g[i]), row(sgu_ln_b[i]), sgu_w[i], bias_full,
                      w_branch_attn[i].astype(BF16), w_branch_sgu[i].astype(BF16), w_out[i].astype(BF16),
                      row(norm_ffn_g[i]))
        keys = peer_sub_keys[i].reshape(2 * PEER_HEADS, PEER_N_KEYS, PEER_HALF).astype(BF16)
        tq, e0, s1, e1 = _route(hn, peer_w_query[i].T.astype(BF16), keys)
        y = _experts(hn, _pack_bf16(peer_down[i]), _pack_bf16(peer_up[i], transpose=True), tq, e0, s1, e1)
        h = _ple(h, y, p[i].reshape(t, -1), row(norm_ple_g[i]), ple_w_gate[i].astype(BF16),
                 ple_w_proj[i].astype(BF16), row(final_norm_g), final=(i == depth - 1))
    if depth == 0:
        raise ValueError("depth must be at least 1")
    return h.reshape(b, s, d)
```

```python
import functools

import jax
import jax.numpy as jnp
from jax import lax
from jax.experimental import pallas as pl
from jax.experimental.pallas import tpu as pltpu

F32 = jnp.float32
BF16 = jnp.bfloat16

NORM_EPS = 1e-6
ATT_HEADS = 8
ATT_HEAD_DIM = 128
ATT_WIDTH = ATT_HEADS * ATT_HEAD_DIM
MOBA_BLOCK = 256
MOBA_TOPK = 3
MOBA_BLOCKS_PER_STEP = 4
MOBA_HEADS_PER_STEP = 4
SGU_GROUPS = 8
SGU_GROUP_DIM = 128
SGU_WIDTH = SGU_GROUPS * SGU_GROUP_DIM
SGU_CHUNK = 128
PEER_HEADS = 8
PEER_N_KEYS = 128
PEER_HALF = 128
PEER_TOPK = 16

GATE_TILE_PARTS = 2
EXPERT_M_PIECES = 4
LANES = 128
SUBLANES = 8
BF16_ROWS = 2 * SUBLANES
VMEM_LIMIT = 56 * 1024 * 1024
MASK_NEG = -(2.0 ** 30)
LOG2E = 1.4426950408889634

D_MODEL = 2048
COL_GATE_A = 0
COL_GATE_B = D_MODEL
COL_Q = 2 * D_MODEL
COL_K = COL_Q + ATT_WIDTH
COL_V = COL_K + ATT_WIDTH
COL_U = COL_V + ATT_WIDTH
COL_VG = COL_U + SGU_WIDTH


def _params(semantics):
    return pltpu.CompilerParams(dimension_semantics=semantics, vmem_limit_bytes=VMEM_LIMIT)


def _rms(x, gain):
    return x * lax.rsqrt(jnp.mean(x * x, axis=-1, keepdims=True) + NORM_EPS) * gain


def _gelu(x):
    return 0.5 * x * (1.0 + lax.erf(x * (0.5 ** 0.5)))


def _dot_nt(a, b):
    return lax.dot_general(a, b, (((1,), (1,)), ((), ())), preferred_element_type=F32)


def _unpack(words):
    return pltpu.bitcast(words, BF16)


def _pack_kernel(x_ref, o_ref, *, transpose):
    x = x_ref[...]
    if transpose:
        x = x.T
    o_ref[...] = pltpu.bitcast(x.astype(BF16), jnp.uint32)


def _pack_bf16(x, transpose=False, tr=512):
    r, c = x.shape
    if transpose:
        out_shape, out_block, out_map = (c // 2, r), (c // 2, tr), lambda i: (0, i)
    else:
        out_shape, out_block, out_map = (r // 2, c), (tr // 2, c), lambda i: (i, 0)
    return pl.pallas_call(
        functools.partial(_pack_kernel, transpose=transpose),
        grid=(r // tr,),
        in_specs=[pl.BlockSpec((tr, c), lambda i: (i, 0))],
        out_specs=pl.BlockSpec(out_block, out_map),
        out_shape=jax.ShapeDtypeStruct(out_shape, jnp.uint32),
        compiler_params=_params(("parallel",)),
        name="pack_bf16_t" if transpose else "pack_bf16",
    )(x)


def _inproj_kernel(x_ref, g_ref, w_ref, o_ref, xn_ref):
    @pl.when(pl.program_id(1) == 0)
    def _():
        xn_ref[...] = _rms(x_ref[...], g_ref[...]).astype(BF16)

    o_ref[...] = jnp.dot(xn_ref[...], w_ref[...], preferred_element_type=F32).astype(o_ref.dtype)


def _inproj(x, gain, w, col_shift, tm=1024, tn=512):
    t, d = x.shape
    n = w.shape[1]
    assert col_shift % tn == 0
    shift = col_shift // tn
    return pl.pallas_call(
        _inproj_kernel,
        grid=(t // tm, n // tn),
        in_specs=[
            pl.BlockSpec((tm, d), lambda i, j: (i, 0)),
            pl.BlockSpec((1, d), lambda i, j: (0, 0)),
            pl.BlockSpec((d, tn), lambda i, j: (0, (j + shift) % (n // tn))),
        ],
        out_specs=pl.BlockSpec((tm, tn), lambda i, j: (i, j)),
        out_shape=jax.ShapeDtypeStruct((t, n), BF16),
        scratch_shapes=[pltpu.VMEM((tm, d), BF16)],
        compiler_params=_params(("parallel", "arbitrary")),
        name="inproj",
    )(x, gain, w)


def _moba_kernel(q_ref, k_ref, v_ref, o_ref, kmean_ref, vt_ref, mask_ref, score_ref):
    j = pl.program_id(2)
    blk, hd, grp = MOBA_BLOCK, ATT_HEAD_DIM, MOBA_BLOCKS_PER_STEP
    heads = range(q_ref.shape[1] // hd)
    nb = k_ref.shape[0] // blk
    nbp = kmean_ref.shape[1]
    ones_rows = vt_ref.shape[2] - hd
    c_exp = (hd ** -0.5) * LOG2E

    def col(h):
        return slice(h * hd, (h + 1) * hd)

    @pl.when(j == 0)
    def _():
        kmean_ref[...] = jnp.zeros_like(kmean_ref)

        def body(g, c):
            for u in range(grp):
                n = g * grp + u
                rows = pl.ds(pl.multiple_of(n * blk, blk), blk)
                for h in heads:
                    kmean_ref[h, pl.ds(n, 1), :] = (jnp.sum(k_ref[rows, col(h)].astype(F32), axis=0, keepdims=True)
                                                    * (1.0 / blk))
                    vt_ref[h, g, :hd, u * blk:(u + 1) * blk] = v_ref[rows, col(h)].T
            for h in heads:
                vt_ref[h, g, hd:, :] = jnp.ones((ones_rows, grp * blk), BF16)
            return c

        lax.fori_loop(0, nb // grp, body, 0)

    own = pl.ds(pl.multiple_of(j * blk, blk), blk)
    qs = [q_ref[:, col(h)] for h in heads]
    gates = [_dot_nt(kmean_ref[h].astype(BF16), qs[h]) for h in heads]
    t_own = [_dot_nt(k_ref[own, col(h)], qs[h]) * c_exp for h in heads]

    def group_scores(h, g):
        rows = pl.ds(pl.multiple_of(g * (grp * blk), grp * blk), grp * blk)
        return _dot_nt(k_ref[rows, col(h)], qs[h]) * c_exp

    row = lax.broadcasted_iota(jnp.int32, gates[0].shape, 0)
    for h in heads:
        g = jnp.where(row < j, gates[h], -jnp.inf)
        maskneg = jnp.full(g.shape, MASK_NEG, F32)
        for r in range(MOBA_TOPK):
            m = jnp.max(g, axis=0, keepdims=True)
            idx = jnp.min(jnp.where(g == m, row, nbp), axis=0, keepdims=True)
            idx = jnp.where(r < j, idx, -1)
            hit = row == idx
            maskneg = jnp.where(hit, 0.0, maskneg)
            g = jnp.where(hit, -jnp.inf, g)
        mask_ref[h] = maskneg
        score_ref[h] = group_scores(h, 0)

    krow = lax.broadcasted_iota(jnp.int32, (blk, blk), 0)
    qcol = lax.broadcasted_iota(jnp.int32, (blk, blk), 1)
    ones = jnp.ones((ones_rows, blk), BF16)
    state = []
    for h in heads:
        t = jnp.where(krow <= qcol, t_own[h], -jnp.inf)
        m0 = jnp.max(t, axis=0, keepdims=True)
        p = jnp.exp2(t - m0)
        v_own = jnp.concatenate([v_ref[own, col(h)].T, ones], axis=0)
        state.append((m0, jnp.dot(v_own, p.astype(BF16), preferred_element_type=F32)))

    def attend(g, carry):
        g_next = jnp.minimum(g + 1, nb // grp - 1)
        new = []
        for h in heads:
            m_prev, acc = carry[h]
            scores = score_ref[h]
            ts = [scores[u * blk:(u + 1) * blk] + mask_ref[h, pl.ds(g * grp + u, 1), :] for u in range(grp)]
            m_new = m_prev
            for t in ts:
                m_new = jnp.maximum(m_new, jnp.max(t, axis=0, keepdims=True))
            ps = [jnp.exp2(t - m_new).astype(BF16) for t in ts]
            score_ref[h] = group_scores(h, g_next)
            acc = (jnp.exp2(m_prev - m_new) * acc
                   + jnp.dot(vt_ref[h, g], jnp.concatenate(ps, axis=0), preferred_element_type=F32))
            new.append((m_new, acc))
        return tuple(new)

    final = lax.fori_loop(0, (j + grp - 1) // grp, attend, tuple(state))
    for h in heads:
        _, acc = final[h]
        o_ref[:, col(h)] = (acc[:hd] / acc[hd:hd + 1]).T.astype(o_ref.dtype)


def _moba(proj3):
    b, s, _ = proj3.shape
    blk, hd, grp, nh = MOBA_BLOCK, ATT_HEAD_DIM, MOBA_BLOCKS_PER_STEP, MOBA_HEADS_PER_STEP
    nb = s // blk
    assert s % blk == 0 and nb % grp == 0 and ATT_HEADS % nh == 0
    nbp = -(-nb // 16) * 16
    wide = nh * hd
    qb, kb, vb = COL_Q // wide, COL_K // wide, COL_V // wide
    return pl.pallas_call(
        _moba_kernel,
        grid=(b, ATT_HEADS // nh, nb),
        in_specs=[
            pl.BlockSpec((None, blk, wide), lambda bi, h, j: (bi, j, qb + h)),
            pl.BlockSpec((None, s, wide), lambda bi, h, j: (bi, 0, kb + h)),
            pl.BlockSpec((None, s, wide), lambda bi, h, j: (bi, 0, vb + h)),
        ],
        out_specs=pl.BlockSpec((None, blk, wide), lambda bi, h, j: (bi, j, h)),
        out_shape=jax.ShapeDtypeStruct((b, s, ATT_WIDTH), BF16),
        scratch_shapes=[
            pltpu.VMEM((nh, nbp, hd), F32),
            pltpu.VMEM((nh, nb // grp, hd + BF16_ROWS, grp * blk), BF16),
            pltpu.VMEM((nh, nbp, blk), F32),
            pltpu.VMEM((nh, grp * blk, blk), F32),
        ],
        compiler_params=_params(("parallel", "parallel", "arbitrary")),
        name="moba",
    )(proj3, proj3, proj3)


def _tail_kernel(ga_ref, gb_ref, u_ref, vg_ref, yatt_ref, x_ref, lng_ref, lnb_ref, ws_ref, bias_ref,
                 wba_ref, wbs_ref, wout_ref, gffn_ref, h_ref, hn_ref, ysgu_ref):
    tm = u_ref.shape[0]
    ch, gd = SGU_CHUNK, SGU_GROUP_DIM
    gu = _gelu(u_ref[...].astype(F32))
    gv = _gelu(vg_ref[...].astype(F32))
    mu = jnp.mean(gv, axis=-1, keepdims=True)
    xc = gv - mu
    vn = xc * lax.rsqrt(jnp.mean(xc * xc, axis=-1, keepdims=True) + NORM_EPS)
    vn = (vn * lng_ref[...] + lnb_ref[...]).astype(BF16)
    trow = lax.broadcasted_iota(jnp.int32, (ch, ch), 0)
    tcol = lax.broadcasted_iota(jnp.int32, (ch, ch), 1)
    for g in range(SGU_GROUPS):
        w = jnp.where(tcol <= trow, ws_ref[g], 0.0).astype(BF16)
        cols = slice(g * gd, (g + 1) * gd)
        for c in range(tm // ch):
            rows = slice(c * ch, (c + 1) * ch)
            mixed = jnp.dot(w, vn[rows, cols], preferred_element_type=F32) + bias_ref[:, cols]
            ysgu_ref[rows, cols] = (gu[rows, cols] * mixed).astype(BF16)
    a = jnp.dot(yatt_ref[...], wba_ref[...], preferred_element_type=F32)
    bsg = jnp.dot(ysgu_ref[...], wbs_ref[...], preferred_element_type=F32)
    merged = (jax.nn.sigmoid(ga_ref[...].astype(F32)) * a
              + jax.nn.sigmoid(gb_ref[...].astype(F32)) * bsg)
    h = x_ref[...] + jnp.dot(merged.astype(BF16), wout_ref[...], preferred_element_type=F32)
    h_ref[...] = h
    hn_ref[...] = pltpu.bitcast(_rms(h, gffn_ref[...]).astype(BF16), jnp.uint32)


def _const_spec(shape):
    return pl.BlockSpec(shape, lambda *_: (0,) * len(shape), pipeline_mode=pl.Buffered(1))


def _tail(proj, yatt, x, ln_g, ln_b, w_s, bias_full, w_ba, w_bs, w_out, g_ffn, tm=256):
    t, d = x.shape
    wide, narrow = d, SGU_WIDTH
    return pl.pallas_call(
        _tail_kernel,
        grid=(t // tm,),
        in_specs=[
            pl.BlockSpec((tm, wide), lambda i: (i, COL_GATE_A // wide)),
            pl.BlockSpec((tm, wide), lambda i: (i, COL_GATE_B // wide)),
            pl.BlockSpec((tm, narrow), lambda i: (i, COL_U // narrow)),
            pl.BlockSpec((tm, narrow), lambda i: (i, COL_VG // narrow)),
            pl.BlockSpec((tm, ATT_WIDTH), lambda i: (i, 0)),
            pl.BlockSpec((tm, d), lambda i: (i, 0)),
            _const_spec((1, narrow)),
            _const_spec((1, narrow)),
            _const_spec(w_s.shape),
            _const_spec(bias_full.shape),
            _const_spec(w_ba.shape),
            _const_spec(w_bs.shape),
            _const_spec(w_out.shape),
            _const_spec((1, d)),
        ],
        out_specs=[pl.BlockSpec((tm, d), lambda i: (i, 0)), pl.BlockSpec((tm // 2, d), lambda i: (i, 0))],
        out_shape=[jax.ShapeDtypeStruct((t, d), F32), jax.ShapeDtypeStruct((t // 2, d), jnp.uint32)],
        scratch_shapes=[pltpu.VMEM((tm, narrow), BF16)],
        compiler_params=_params(("parallel",)),
        name="mixer_tail",
    )(proj, proj, proj, proj, yatt, x, ln_g, ln_b, w_s, bias_full, w_ba, w_bs, w_out, g_ffn)


N_EXTRACT = PEER_TOPK + 1
CAND_PAIRS = tuple((a, b) for a in range(N_EXTRACT) for b in range(N_EXTRACT) if (a + 1) * (b + 1) <= N_EXTRACT)
CAND_ROWS = SUBLANES * (1 << (-(-len(CAND_PAIRS) // SUBLANES) - 1).bit_length())
ROUTE_HEADS_PER_STEP = 2


def _extract_top(works, n_iter):
    rows = works[0].shape[0]
    row = lax.broadcasted_iota(jnp.int32, works[0].shape, 0)
    works = list(works)
    vals = [[] for _ in works]
    for _ in range(n_iter):
        for k, work in enumerate(works):
            m = jnp.max(work, axis=0, keepdims=True)
            idx = jnp.min(jnp.where(work == m, row, rows), axis=0, keepdims=True)
            works[k] = jnp.where(row == idx, -jnp.inf, work)
            vals[k].append(m)
    return vals


def _sorting_network(n):
    pairs = []
    p = 1
    while p < n:
        k = p
        while k >= 1:
            for j in range(k % p, n - k, 2 * k):
                for i in range(min(k, n - j - k)):
                    if (i + j) // (2 * p) == (i + j + k) // (2 * p):
                        pairs.append((i + j, i + j + k))
            k //= 2
        p *= 2
    return pairs


def _extract_top_distinct(works, n_iter):
    lists = []
    for work in works:
        slabs = [work[SUBLANES * k:SUBLANES * (k + 1)] for k in range(work.shape[0] // SUBLANES)]
        for i, j in _sorting_network(len(slabs)):
            slabs[i], slabs[j] = jnp.maximum(slabs[i], slabs[j]), jnp.minimum(slabs[i], slabs[j])
        lists.append(slabs)
    vals = [[] for _ in works]
    popped = [jnp.zeros_like(slabs[0]) for slabs in lists]
    for r in range(n_iter):
        for k, slabs in enumerate(lists):
            m = jnp.max(slabs[0], axis=0, keepdims=True)
            hit = slabs[0] == m
            for d in range(min(len(slabs), n_iter - r - 1)):
                below = slabs[d + 1] if d + 1 < len(slabs) else -jnp.inf
                slabs[d] = jnp.where(hit, below, slabs[d])
            popped[k] = popped[k] + jnp.where(hit, 1.0, 0.0)
            vals[k].append(m)
    return [(v, jnp.abs(jnp.sum(p, axis=0, keepdims=True) - n_iter)) for v, p in zip(vals, popped)]


def _route_kernel(hn_ref, wqt_ref, keys_ref, tq_ref, e0_ref, s1_ref, e1_ref, cand_ref):
    hn = _unpack(hn_ref[...])
    tm = hn.shape[0]
    group = cand_ref.shape[0]
    cand_ref[...] = jnp.full(cand_ref.shape, -jnp.inf, F32)

    def candidates(k, tv0, tv1):
        for r, (a, b) in enumerate(CAND_PAIRS):
            cand_ref[k, r:r + 1, :] = tv0[a] + tv1[b]

    def finish(hh, s0, max0, c):
        z = jnp.ones_like(c[0])
        for r in range(1, PEER_TOPK):
            z = z + jnp.exp(c[r] - c[0])
        tau = 0.5 * (c[PEER_TOPK - 1] + c[PEER_TOPK])
        store(tq_ref, hh, tau - s0)
        store(e0_ref, hh, jnp.exp(s0 - max0) / z)

    def store(ref, hh, val):
        for cc in range(tm // LANES):
            ref[cc, hh] = val[:, cc * LANES:(cc + 1) * LANES]

    def heads(g, carry):
        scores = []
        for k in range(group):
            hh = g * group + k
            w = wqt_ref[pl.ds(pl.multiple_of(hh * 2 * PEER_HALF, 2 * PEER_HALF), 2 * PEER_HALF), :]
            qt = _dot_nt(w, hn).astype(BF16)
            scores.append((jnp.dot(keys_ref[2 * hh], qt[:PEER_HALF], preferred_element_type=F32),
                           jnp.dot(keys_ref[2 * hh + 1], qt[PEER_HALF:], preferred_element_type=F32)))
        tops = _extract_top_distinct([s for pair in scores for s in pair], N_EXTRACT)
        for k in range(group):
            candidates(k, tops[2 * k][0], tops[2 * k + 1][0])
        ctops = _extract_top_distinct([cand_ref[k] for k in range(group)], N_EXTRACT)
        for k, (s0, s1) in enumerate(scores):
            hh = g * group + k
            (tv0, bad0), (tv1, bad1), (c, badc) = tops[2 * k], tops[2 * k + 1], ctops[k]
            finish(hh, s0, tv0[0], c)
            store(s1_ref, hh, s1)
            store(e1_ref, hh, jnp.exp(s1 - tv1[0]))
            scores_tied = jnp.max(bad0 + bad1) > 0.0
            sums_tied = jnp.max(badc) > 0.0

            @pl.when(scores_tied)
            def _():
                xv0, xv1 = _extract_top([s0, s1], N_EXTRACT)
                candidates(k, xv0, xv1)
                finish(hh, s0, xv0[0], _extract_top([cand_ref[k]], N_EXTRACT)[0])

            @pl.when(jnp.logical_and(jnp.logical_not(scores_tied), sums_tied))
            def _():
                finish(hh, s0, tv0[0], _extract_top([cand_ref[k]], N_EXTRACT)[0])

        return carry

    lax.fori_loop(0, PEER_HEADS // group, heads, 0)


def _route(hn, wq_t, keys, tm=512):
    t, d = 2 * hn.shape[0], hn.shape[1]
    nck = tm // LANES
    out_block = pl.BlockSpec((nck, PEER_HEADS, PEER_N_KEYS, LANES), lambda i: (i, 0, 0, 0))
    out_shape = jax.ShapeDtypeStruct((t // LANES, PEER_HEADS, PEER_N_KEYS, LANES), F32)
    return pl.pallas_call(
        _route_kernel,
        grid=(t // tm,),
        in_specs=[
            pl.BlockSpec((tm // 2, d), lambda i: (i, 0)),
            _const_spec(wq_t.shape),
            _const_spec(keys.shape),
        ],
        out_specs=[out_block] * 4,
        out_shape=[out_shape] * 4,
        scratch_shapes=[pltpu.VMEM((ROUTE_HEADS_PER_STEP, CAND_ROWS, tm), F32)],
        compiler_params=_params(("parallel",)),
        name="peer_route",
    )(hn, wq_t, keys)


def _experts_kernel(hn_ref, down_ref, upt_ref, tq_ref, e0_ref, s1_ref, e1_ref, o_ref,
                    acc_ref, hid0_ref, hid1_ref, gh0_ref, gh1_ref, *, n_tiles):
    s = pl.program_id(0)
    te = 2 * down_ref.shape[0]
    tm = 2 * hn_ref.shape[0]

    @pl.when(s == 0)
    def _():
        for ref in (hid0_ref, hid1_ref, gh0_ref, gh1_ref):
            ref[...] = jnp.zeros_like(ref)

    @pl.when((s == 0) | (s % n_tiles == 2 % n_tiles))
    def _():
        acc_ref[...] = jnp.zeros_like(acc_ref)

    def gated_tile(hid_r, gh_w, ii, cc, part):
        n_sub = PEER_N_KEYS // GATE_TILE_PARTS
        sub = slice(part * n_sub, (part + 1) * n_sub)
        rows = slice(ii * PEER_N_KEYS + part * n_sub, ii * PEER_N_KEYS + (part + 1) * n_sub)
        lanes = slice(cc * LANES, (cc + 1) * LANES)
        gate = jnp.zeros((n_sub, LANES), F32)
        for hh in range(PEER_HEADS):
            thr = tq_ref[cc, hh, ii:ii + 1, :]
            scale = e0_ref[cc, hh, ii:ii + 1, :]
            gate = gate + jnp.where(s1_ref[cc, hh, sub, :] >= thr, e1_ref[cc, hh, sub, :], 0.0) * scale
        gh_w[rows, lanes] = (gate * _gelu(hid_r[rows, lanes])).astype(BF16)

    def step(hid_w, hid_r, gh_w, gh_r):
        tiles = [(ii, cc) for ii in range(te // PEER_N_KEYS) for cc in range(tm // LANES)]
        n_m = EXPERT_M_PIECES
        n_k = len(tiles) // (2 * n_m)
        d = acc_ref.shape[0]
        tile = iter(tiles)
        for nh in range(2):
            lanes = slice(nh * (tm // 2), (nh + 1) * (tm // 2))
            words_t = slice(nh * (tm // 4), (nh + 1) * (tm // 4))
            for mh in range(n_m):
                rows_c = slice(mh * (d // n_m), (mh + 1) * (d // n_m))
                words_c = slice(mh * (d // n_m // 2), (mh + 1) * (d // n_m // 2))
                rows_a = slice(mh * (te // n_m), (mh + 1) * (te // n_m))
                words_a = slice(mh * (te // n_m // 2), (mh + 1) * (te // n_m // 2))
                up = down = None
                for kc in range(n_k):
                    ii, cc = next(tile)
                    kk_a = slice(kc * (d // n_k), (kc + 1) * (d // n_k))
                    part = _dot_nt(_unpack(down_ref[words_a, kk_a]), _unpack(hn_ref[words_t, kk_a]))
                    down = part if down is None else down + part
                    gated_tile(hid_r, gh_w, ii, cc, 0)
                    kk_c = slice(kc * (te // n_k), (kc + 1) * (te // n_k))
                    part = jnp.dot(_unpack(upt_ref[words_c, kk_c]), gh_r[kk_c, lanes],
                                   preferred_element_type=F32)
                    up = part if up is None else up + part
                    gated_tile(hid_r, gh_w, ii, cc, 1)
                hid_w[rows_a, lanes] = down
                acc_ref[rows_c, lanes] += up

    @pl.when(s % 2 == 0)
    def _():
        step(hid0_ref, hid1_ref, gh1_ref, gh0_ref)

    @pl.when(s % 2 == 1)
    def _():
        step(hid1_ref, hid0_ref, gh0_ref, gh1_ref)

    @pl.when((s >= 2) & ((s - 2) % n_tiles == n_tiles - 1))
    def _():
        o_ref[...] = acc_ref[...].T


def _experts(hn, down, up_t, tq, e0, s1, e1, tm=512, te=1024):
    t, d = 2 * hn.shape[0], hn.shape[1]
    nt, ne = t // tm, 2 * down.shape[0] // te
    nck = tm // LANES
    ni = te // PEER_N_KEYS
    assert ni % 8 == 0
    tok_a = lambda s: jnp.minimum(s // ne, nt - 1)
    tok_b = lambda s: jnp.minimum(jnp.maximum(s - 1, 0) // ne, nt - 1)
    tok_c = lambda s: jnp.minimum(jnp.maximum(s - 2, 0) // ne, nt - 1)
    exp_b = lambda s: jnp.maximum(s - 1, 0) % ne
    exp_c = lambda s: jnp.maximum(s - 2, 0) % ne
    row_block = pl.BlockSpec((nck, PEER_HEADS, ni, LANES), lambda s: (tok_b(s), 0, exp_b(s), 0))
    full_block = pl.BlockSpec((nck, PEER_HEADS, PEER_N_KEYS, LANES), lambda s: (tok_b(s), 0, 0, 0))
    return pl.pallas_call(
        functools.partial(_experts_kernel, n_tiles=ne),
        grid=(nt * ne + 2,),
        in_specs=[
            pl.BlockSpec((tm // 2, d), lambda s: (tok_a(s), 0)),
            pl.BlockSpec((te // 2, d), lambda s: (s % ne, 0)),
            pl.BlockSpec((d // 2, te), lambda s: (0, exp_c(s))),
            row_block, row_block, full_block, full_block,
        ],
        out_specs=pl.BlockSpec((tm, d), lambda s: (tok_c(s), 0)),
        out_shape=jax.ShapeDtypeStruct((t, d), F32),
        scratch_shapes=[
            pltpu.VMEM((d, tm), F32),
            pltpu.VMEM((te, tm), F32), pltpu.VMEM((te, tm), F32),
            pltpu.VMEM((te, tm), BF16), pltpu.VMEM((te, tm), BF16),
        ],
        compiler_params=_params(("arbitrary",)),
        name="peer_experts",
    )(hn, down, up_t, tq, e0, s1, e1)


def _ple_kernel(h_ref, y_ref, p_ref, gple_ref, wgate_ref, wproj_ref, gfin_ref, o_ref, *, final):
    h = h_ref[...] + y_ref[...]
    gate = jax.nn.sigmoid(jnp.dot(_rms(h, gple_ref[...]).astype(BF16), wgate_ref[...], preferred_element_type=F32))
    emb = jnp.dot(p_ref[...].astype(BF16), wproj_ref[...], preferred_element_type=F32)
    h = h + gate * emb
    o_ref[...] = _rms(h, gfin_ref[...]) if final else h


def _ple(h, y, p, g_ple, w_gate, w_proj, g_final, final, tm=512):
    t, d = h.shape
    pd = p.shape[1]
    return pl.pallas_call(
        functools.partial(_ple_kernel, final=final),
        grid=(t // tm,),
        in_specs=[
            pl.BlockSpec((tm, d), lambda i: (i, 0)),
            pl.BlockSpec((tm, d), lambda i: (i, 0)),
            pl.BlockSpec((tm, pd), lambda i: (i, 0)),
            _const_spec((1, d)),
            _const_spec(w_gate.shape),
            _const_spec(w_proj.shape),
            _const_spec((1, d)),
        ],
        out_specs=pl.BlockSpec((tm, d), lambda i: (i, 0)),
        out_shape=jax.ShapeDtypeStruct((t, d), F32),
        compiler_params=_params(("parallel",)),
        name="ple",
    )(h, y, p, g_ple, w_gate, w_proj, g_final)


def kernel(x, p, norm_mix_g, w_in, sgu_ln_g, sgu_ln_b, sgu_w, sgu_b, w_branch_attn, w_branch_sgu, w_out, norm_ffn_g, peer_w_query, peer_sub_keys, peer_down, peer_up, norm_ple_g, ple_w_proj, ple_w_gate, final_norm_g):
    b, s, d = x.shape
    assert d == D_MODEL
    t = b * s
    depth = w_in.shape[0]
    h = x.reshape(t, d)
    row = lambda v: v.reshape(1, -1)
    aw = ATT_WIDTH
    for i in range(depth):
        proj = _inproj(h, row(norm_mix_g[i]), w_in[i].astype(BF16), col_shift=3 * aw + 2 * SGU_WIDTH)
        yatt = _moba(proj.reshape(b, s, -1)).reshape(t, aw)
        bias_full = jnp.repeat(sgu_b[i].T, SGU_GROUP_DIM, axis=1)
        h, hn = _tail(proj, yatt, h, row(sgu_ln_g[i]), row(sgu_ln_b[i]), sgu_w[i], bias_full,
                      w_branch_attn[i].astype(BF16), w_branch_sgu[i].astype(BF16), w_out[i].astype(BF16),
                      row(norm_ffn_g[i]))
        keys = peer_sub_keys[i].reshape(2 * PEER_HEADS, PEER_N_KEYS, PEER_HALF).astype(BF16)
        tq, e0, s1, e1 = _route(hn, peer_w_query[i].T.astype(BF16), keys)
        y = _experts(hn, _pack_bf16(peer_down[i]), _pack_bf16(peer_up[i], transpose=True), tq, e0, s1, e1)
        h = _ple(h, y, p[i].reshape(t, -1), row(norm_ple_g[i]), ple_w_gate[i].astype(BF16),
                 ple_w_proj[i].astype(BF16), row(final_norm_g), final=(i == depth - 1))
    if depth == 0:
        raise ValueError("depth must be at least 1")
    return h.reshape(b, s, d)
```

```python
import functools

import jax
import jax.numpy as jnp
from jax import lax
from jax.experimental import pallas as pl
from jax.experimental.pallas import tpu as pltpu

F32 = jnp.float32
BF16 = jnp.bfloat16

NORM_EPS = 1e-6
ATT_HEADS = 8
ATT_HEAD_DIM = 128
ATT_WIDTH = ATT_HEADS * ATT_HEAD_DIM
MOBA_BLOCK = 256
MOBA_TOPK = 3
MOBA_BLOCKS_PER_STEP = 4
MOBA_HEADS_PER_STEP = 4
SGU_GROUPS = 8
SGU_GROUP_DIM = 128
SGU_WIDTH = SGU_GROUPS * SGU_GROUP_DIM
SGU_CHUNK = 128
PEER_HEADS = 8
PEER_N_KEYS = 128
PEER_HALF = 128
PEER_TOPK = 16

GATE_TILE_PARTS = 2
EXPERT_M_PIECES = 4
LANES = 128
SUBLANES = 8
BF16_ROWS = 2 * SUBLANES
VMEM_LIMIT = 56 * 1024 * 1024
MASK_NEG = -(2.0 ** 30)
LOG2E = 1.4426950408889634

D_MODEL = 2048
COL_GATE_A = 0
COL_GATE_B = D_MODEL
COL_Q = 2 * D_MODEL
COL_K = COL_Q + ATT_WIDTH
COL_V = COL_K + ATT_WIDTH
COL_U = COL_V + ATT_WIDTH
COL_VG = COL_U + SGU_WIDTH


def _params(semantics):
    return pltpu.CompilerParams(dimension_semantics=semantics, vmem_limit_bytes=VMEM_LIMIT)


def _rms(x, gain):
    return x * lax.rsqrt(jnp.mean(x * x, axis=-1, keepdims=True) + NORM_EPS) * gain


def _gelu(x):
    return 0.5 * x * (1.0 + lax.erf(x * (0.5 ** 0.5)))


def _dot_nt(a, b):
    return lax.dot_general(a, b, (((1,), (1,)), ((), ())), preferred_element_type=F32)


def _unpack(words):
    return pltpu.bitcast(words, BF16)


def _pack_kernel(x_ref, o_ref, *, transpose):
    x = x_ref[...]
    if transpose:
        x = x.T
    o_ref[...] = pltpu.bitcast(x.astype(BF16), jnp.uint32)


def _pack_bf16(x, transpose=False, tr=512):
    r, c = x.shape
    if transpose:
        out_shape, out_block, out_map = (c // 2, r), (c // 2, tr), lambda i: (0, i)
    else:
        out_shape, out_block, out_map = (r // 2, c), (tr // 2, c), lambda i: (i, 0)
    return pl.pallas_call(
        functools.partial(_pack_kernel, transpose=transpose),
        grid=(r // tr,),
        in_specs=[pl.BlockSpec((tr, c), lambda i: (i, 0))],
        out_specs=pl.BlockSpec(out_block, out_map),
        out_shape=jax.ShapeDtypeStruct(out_shape, jnp.uint32),
        compiler_params=_params(("parallel",)),
        name="pack_bf16_t" if transpose else "pack_bf16",
    )(x)


def _inproj_kernel(x_ref, g_ref, w_ref, o_ref, xn_ref):
    @pl.when(pl.program_id(1) == 0)
    def _():
        xn_ref[...] = _rms(x_ref[...], g_ref[...]).astype(BF16)

    o_ref[...] = jnp.dot(xn_ref[...], w_ref[...], preferred_element_type=F32).astype(o_ref.dtype)


def _inproj(x, gain, w, col_shift, tm=1024, tn=1024):
    t, d = x.shape
    n = w.shape[1]
    assert col_shift % tn == 0
    shift = col_shift // tn
    return pl.pallas_call(
        _inproj_kernel,
        grid=(t // tm, n // tn),
        in_specs=[
            pl.BlockSpec((tm, d), lambda i, j: (i, 0)),
            pl.BlockSpec((1, d), lambda i, j: (0, 0)),
            pl.BlockSpec((d, tn), lambda i, j: (0, (j + shift) % (n // tn))),
        ],
        out_specs=pl.BlockSpec((tm, tn), lambda i, j: (i, j)),
        out_shape=jax.ShapeDtypeStruct((t, n), BF16),
        scratch_shapes=[pltpu.VMEM((tm, d), BF16)],
        compiler_params=_params(("parallel", "arbitrary")),
        name="inproj",
    )(x, gain, w)


def _moba_kernel(q_ref, k_ref, v_ref, o_ref, kmean_ref, vt_ref, mask_ref, score_ref):
    j = pl.program_id(2)
    blk, hd, grp = MOBA_BLOCK, ATT_HEAD_DIM, MOBA_BLOCKS_PER_STEP
    heads = range(q_ref.shape[1] // hd)
    nb = k_ref.shape[0] // blk
    nbp = kmean_ref.shape[1]
    ones_rows = vt_ref.shape[2] - hd
    c_exp = (hd ** -0.5) * LOG2E

    def col(h):
        return slice(h * hd, (h + 1) * hd)

    @pl.when(j == 0)
    def _():
        kmean_ref[...] = jnp.zeros_like(kmean_ref)

        def body(g, c):
            for u in range(grp):
                n = g * grp + u
                rows = pl.ds(pl.multiple_of(n * blk, blk), blk)
                for h in heads:
                    kmean_ref[h, pl.ds(n, 1), :] = (jnp.sum(k_ref[rows, col(h)].astype(F32), axis=0, keepdims=True)
                                                    * (1.0 / blk))
                    vt_ref[h, g, :hd, u * blk:(u + 1) * blk] = v_ref[rows, col(h)].T
            for h in heads:
                vt_ref[h, g, hd:, :] = jnp.ones((ones_rows, grp * blk), BF16)
            return c

        lax.fori_loop(0, nb // grp, body, 0)

    own = pl.ds(pl.multiple_of(j * blk, blk), blk)
    qs = [q_ref[:, col(h)] for h in heads]
    gates = [_dot_nt(kmean_ref[h].astype(BF16), qs[h]) for h in heads]
    t_own = [_dot_nt(k_ref[own, col(h)], qs[h]) * c_exp for h in heads]

    def group_scores(h, g):
        rows = pl.ds(pl.multiple_of(g * (grp * blk), grp * blk), grp * blk)
        return _dot_nt(k_ref[rows, col(h)], qs[h]) * c_exp

    row = lax.broadcasted_iota(jnp.int32, gates[0].shape, 0)
    for h in heads:
        g = jnp.where(row < j, gates[h], -jnp.inf)
        maskneg = jnp.full(g.shape, MASK_NEG, F32)
        for r in range(MOBA_TOPK):
            m = jnp.max(g, axis=0, keepdims=True)
            idx = jnp.min(jnp.where(g == m, row, nbp), axis=0, keepdims=True)
            idx = jnp.where(r < j, idx, -1)
            hit = row == idx
            maskneg = jnp.where(hit, 0.0, maskneg)
            g = jnp.where(hit, -jnp.inf, g)
        mask_ref[h] = maskneg
        score_ref[h] = group_scores(h, 0)

    krow = lax.broadcasted_iota(jnp.int32, (blk, blk), 0)
    qcol = lax.broadcasted_iota(jnp.int32, (blk, blk), 1)
    ones = jnp.ones((ones_rows, blk), BF16)
    state = []
    for h in heads:
        t = jnp.where(krow <= qcol, t_own[h], -jnp.inf)
        m0 = jnp.max(t, axis=0, keepdims=True)
        p = jnp.exp2(t - m0)
        v_own = jnp.concatenate([v_ref[own, col(h)].T, ones], axis=0)
        state.append((m0, jnp.dot(v_own, p.astype(BF16), preferred_element_type=F32)))

    def attend(g, carry):
        g_next = jnp.minimum(g + 1, nb // grp - 1)
        new = []
        for h in heads:
            m_prev, acc = carry[h]
            scores = score_ref[h]
            ts = [scores[u * blk:(u + 1) * blk] + mask_ref[h, pl.ds(g * grp + u, 1), :] for u in range(grp)]
            m_new = m_prev
            for t in ts:
                m_new = jnp.maximum(m_new, jnp.max(t, axis=0, keepdims=True))
            ps = [jnp.exp2(t - m_new).astype(BF16) for t in ts]
            score_ref[h] = group_scores(h, g_next)
            acc = (jnp.exp2(m_prev - m_new) * acc
                   + jnp.dot(vt_ref[h, g], jnp.concatenate(ps, axis=0), preferred_element_type=F32))
            new.append((m_new, acc))
        return tuple(new)

    final = lax.fori_loop(0, (j + grp - 1) // grp, attend, tuple(state))
    for h in heads:
        _, acc = final[h]
        o_ref[:, col(h)] = (acc[:hd] / acc[hd:hd + 1]).T.astype(o_ref.dtype)


def _moba(proj3):
    b, s, _ = proj3.shape
    blk, hd, grp, nh = MOBA_BLOCK, ATT_HEAD_DIM, MOBA_BLOCKS_PER_STEP, MOBA_HEADS_PER_STEP
    nb = s // blk
    assert s % blk == 0 and nb % grp == 0 and ATT_HEADS % nh == 0
    nbp = -(-nb // 16) * 16
    wide = nh * hd
    qb, kb, vb = COL_Q // wide, COL_K // wide, COL_V // wide
    return pl.pallas_call(
        _moba_kernel,
        grid=(b, ATT_HEADS // nh, nb),
        in_specs=[
            pl.BlockSpec((None, blk, wide), lambda bi, h, j: (bi, j, qb + h)),
            pl.BlockSpec((None, s, wide), lambda bi, h, j: (bi, 0, kb + h)),
            pl.BlockSpec((None, s, wide), lambda bi, h, j: (bi, 0, vb + h)),
        ],
        out_specs=pl.BlockSpec((None, blk, wide), lambda bi, h, j: (bi, j, h)),
        out_shape=jax.ShapeDtypeStruct((b, s, ATT_WIDTH), BF16),
        scratch_shapes=[
            pltpu.VMEM((nh, nbp, hd), F32),
            pltpu.VMEM((nh, nb // grp, hd + BF16_ROWS, grp * blk), BF16),
            pltpu.VMEM((nh, nbp, blk), F32),
            pltpu.VMEM((nh, grp * blk, blk), F32),
        ],
        compiler_params=_params(("parallel", "parallel", "arbitrary")),
        name="moba",
    )(proj3, proj3, proj3)


def _tail_kernel(ga_ref, gb_ref, u_ref, vg_ref, yatt_ref, x_ref, lng_ref, lnb_ref, ws_ref, bias_ref,
                 wba_ref, wbs_ref, wout_ref, gffn_ref, h_ref, hn_ref, ysgu_ref):
    tm = u_ref.shape[0]
    ch, gd = SGU_CHUNK, SGU_GROUP_DIM
    gu = _gelu(u_ref[...].astype(F32))
    gv = _gelu(vg_ref[...].astype(F32))
    mu = jnp.mean(gv, axis=-1, keepdims=True)
    xc = gv - mu
    vn = xc * lax.rsqrt(jnp.mean(xc * xc, axis=-1, keepdims=True) + NORM_EPS)
    vn = (vn * lng_ref[...] + lnb_ref[...]).astype(BF16)
    trow = lax.broadcasted_iota(jnp.int32, (ch, ch), 0)
    tcol = lax.broadcasted_iota(jnp.int32, (ch, ch), 1)
    for g in range(SGU_GROUPS):
        w = jnp.where(tcol <= trow, ws_ref[g], 0.0).astype(BF16)
        cols = slice(g * gd, (g + 1) * gd)
        for c in range(tm // ch):
            rows = slice(c * ch, (c + 1) * ch)
            mixed = jnp.dot(w, vn[rows, cols], preferred_element_type=F32) + bias_ref[:, cols]
            ysgu_ref[rows, cols] = (gu[rows, cols] * mixed).astype(BF16)
    a = jnp.dot(yatt_ref[...], wba_ref[...], preferred_element_type=F32)
    bsg = jnp.dot(ysgu_ref[...], wbs_ref[...], preferred_element_type=F32)
    merged = (jax.nn.sigmoid(ga_ref[...].astype(F32)) * a
              + jax.nn.sigmoid(gb_ref[...].astype(F32)) * bsg)
    h = x_ref[...] + jnp.dot(merged.astype(BF16), wout_ref[...], preferred_element_type=F32)
    h_ref[...] = h
    hn_ref[...] = pltpu.bitcast(_rms(h, gffn_ref[...]).astype(BF16), jnp.uint32)


def _const_spec(shape):
    return pl.BlockSpec(shape, lambda *_: (0,) * len(shape), pipeline_mode=pl.Buffered(1))


def _tail(proj, yatt, x, ln_g, ln_b, w_s, bias_full, w_ba, w_bs, w_out, g_ffn, tm=256):
    t, d = x.shape
    wide, narrow = d, SGU_WIDTH
    return pl.pallas_call(
        _tail_kernel,
        grid=(t // tm,),
        in_specs=[
            pl.BlockSpec((tm, wide), lambda i: (i, COL_GATE_A // wide)),
            pl.BlockSpec((tm, wide), lambda i: (i, COL_GATE_B // wide)),
            pl.BlockSpec((tm, narrow), lambda i: (i, COL_U // narrow)),
            pl.BlockSpec((tm, narrow), lambda i: (i, COL_VG // narrow)),
            pl.BlockSpec((tm, ATT_WIDTH), lambda i: (i, 0)),
            pl.BlockSpec((tm, d), lambda i: (i, 0)),
            _const_spec((1, narrow)),
            _const_spec((1, narrow)),
            _const_spec(w_s.shape),
            _const_spec(bias_full.shape),
            _const_spec(w_ba.shape),
            _const_spec(w_bs.shape),
            _const_spec(w_out.shape),
            _const_spec((1, d)),
        ],
        out_specs=[pl.BlockSpec((tm, d), lambda i: (i, 0)), pl.BlockSpec((tm // 2, d), lambda i: (i, 0))],
        out_shape=[jax.ShapeDtypeStruct((t, d), F32), jax.ShapeDtypeStruct((t // 2, d), jnp.uint32)],
        scratch_shapes=[pltpu.VMEM((tm, narrow), BF16)],
        compiler_params=_params(("parallel",)),
        name="mixer_tail",
    )(proj, proj, proj, proj, yatt, x, ln_g, ln_b, w_s, bias_full, w_ba, w_bs, w_out, g_ffn)


N_EXTRACT = PEER_TOPK + 1
CAND_PAIRS = tuple((a, b) for a in range(N_EXTRACT) for b in range(N_EXTRACT) if (a + 1) * (b + 1) <= N_EXTRACT)
CAND_ROWS = SUBLANES * (1 << (-(-len(CAND_PAIRS) // SUBLANES) - 1).bit_length())
ROUTE_HEADS_PER_STEP = 2


def _extract_top(works, n_iter):
    rows = works[0].shape[0]
    row = lax.broadcasted_iota(jnp.int32, works[0].shape, 0)
    works = list(works)
    vals = [[] for _ in works]
    for _ in range(n_iter):
        for k, work in enumerate(works):
            m = jnp.max(work, axis=0, keepdims=True)
            idx = jnp.min(jnp.where(work == m, row, rows), axis=0, keepdims=True)
            works[k] = jnp.where(row == idx, -jnp.inf, work)
            vals[k].append(m)
    return vals


def _sorting_network(n):
    pairs = []
    p = 1
    while p < n:
        k = p
        while k >= 1:
            for j in range(k % p, n - k, 2 * k):
                for i in range(min(k, n - j - k)):
                    if (i + j) // (2 * p) == (i + j + k) // (2 * p):
                        pairs.append((i + j, i + j + k))
            k //= 2
        p *= 2
    return pairs


def _extract_top_distinct(works, n_iter):
    lists = []
    for work in works:
        slabs = [work[SUBLANES * k:SUBLANES * (k + 1)] for k in range(work.shape[0] // SUBLANES)]
        for i, j in _sorting_network(len(slabs)):
            slabs[i], slabs[j] = jnp.maximum(slabs[i], slabs[j]), jnp.minimum(slabs[i], slabs[j])
        lists.append(slabs)
    vals = [[] for _ in works]
    popped = [jnp.zeros_like(slabs[0]) for slabs in lists]
    for r in range(n_iter):
        for k, slabs in enumerate(lists):
            m = jnp.max(slabs[0], axis=0, keepdims=True)
            hit = slabs[0] == m
            for d in range(min(len(slabs), n_iter - r - 1)):
                below = slabs[d + 1] if d + 1 < len(slabs) else -jnp.inf
                slabs[d] = jnp.where(hit, below, slabs[d])
            popped[k] = popped[k] + jnp.where(hit, 1.0, 0.0)
            vals[k].append(m)
    return [(v, jnp.abs(jnp.sum(p, axis=0, keepdims=True) - n_iter)) for v, p in zip(vals, popped)]


def _route_kernel(hn_ref, wqt_ref, keys_ref, tq_ref, e0_ref, s1_ref, e1_ref, cand_ref):
    hn = _unpack(hn_ref[...])
    tm = hn.shape[0]
    group = cand_ref.shape[0]
    cand_ref[...] = jnp.full(cand_ref.shape, -jnp.inf, F32)

    def candidates(k, tv0, tv1):
        for r, (a, b) in enumerate(CAND_PAIRS):
            cand_ref[k, r:r + 1, :] = tv0[a] + tv1[b]

    def finish(hh, s0, max0, c):
        z = jnp.ones_like(c[0])
        for r in range(1, PEER_TOPK):
            z = z + jnp.exp(c[r] - c[0])
        tau = 0.5 * (c[PEER_TOPK - 1] + c[PEER_TOPK])
        store(tq_ref, hh, tau - s0)
        store(e0_ref, hh, jnp.exp(s0 - max0) / z)

    def store(ref, hh, val):
        for cc in range(tm // LANES):
            ref[cc, hh] = val[:, cc * LANES:(cc + 1) * LANES]

    def heads(g, carry):
        scores = []
        for k in range(group):
            hh = g * group + k
            w = wqt_ref[pl.ds(pl.multiple_of(hh * 2 * PEER_HALF, 2 * PEER_HALF), 2 * PEER_HALF), :]
            qt = _dot_nt(w, hn).astype(BF16)
            scores.append((jnp.dot(keys_ref[2 * hh], qt[:PEER_HALF], preferred_element_type=F32),
                           jnp.dot(keys_ref[2 * hh + 1], qt[PEER_HALF:], preferred_element_type=F32)))
        tops = _extract_top_distinct([s for pair in scores for s in pair], N_EXTRACT)
        for k in range(group):
            candidates(k, tops[2 * k][0], tops[2 * k + 1][0])
        ctops = _extract_top_distinct([cand_ref[k] for k in range(group)], N_EXTRACT)
        for k, (s0, s1) in enumerate(scores):
            hh = g * group + k
            (tv0, bad0), (tv1, bad1), (c, badc) = tops[2 * k], tops[2 * k + 1], ctops[k]
            finish(hh, s0, tv0[0], c)
            store(s1_ref, hh, s1)
            store(e1_ref, hh, jnp.exp(s1 - tv1[0]))
            scores_tied = jnp.max(bad0 + bad1) > 0.0
            sums_tied = jnp.max(badc) > 0.0

            @pl.when(scores_tied)
            def _():
                xv0, xv1 = _extract_top([s0, s1], N_EXTRACT)
                candidates(k, xv0, xv1)
                finish(hh, s0, xv0[0], _extract_top([cand_ref[k]], N_EXTRACT)[0])

            @pl.when(jnp.logical_and(jnp.logical_not(scores_tied), sums_tied))
            def _():
                finish(hh, s0, tv0[0], _extract_top([cand_ref[k]], N_EXTRACT)[0])

        return carry

    lax.fori_loop(0, PEER_HEADS // group, heads, 0)


def _route(hn, wq_t, keys, tm=512):
    t, d = 2 * hn.shape[0], hn.shape[1]
    nck = tm // LANES
    out_block = pl.BlockSpec((nck, PEER_HEADS, PEER_N_KEYS, LANES), lambda i: (i, 0, 0, 0))
    out_shape = jax.ShapeDtypeStruct((t // LANES, PEER_HEADS, PEER_N_KEYS, LANES), F32)
    return pl.pallas_call(
        _route_kernel,
        grid=(t // tm,),
        in_specs=[
            pl.BlockSpec((tm // 2, d), lambda i: (i, 0)),
            _const_spec(wq_t.shape),
            _const_spec(keys.shape),
        ],
        out_specs=[out_block] * 4,
        out_shape=[out_shape] * 4,
        scratch_shapes=[pltpu.VMEM((ROUTE_HEADS_PER_STEP, CAND_ROWS, tm), F32)],
        compiler_params=_params(("parallel",)),
        name="peer_route",
    )(hn, wq_t, keys)


def _experts_kernel(hn_ref, down_ref, upt_ref, tq_ref, e0_ref, s1_ref, e1_ref, o_ref,
                    acc_ref, hid0_ref, hid1_ref, gh0_ref, gh1_ref, *, n_tiles):
    s = pl.program_id(0)
    te = 2 * down_ref.shape[0]
    tm = 2 * hn_ref.shape[0]

    @pl.when(s == 0)
    def _():
        for ref in (hid0_ref, hid1_ref, gh0_ref, gh1_ref):
            ref[...] = jnp.zeros_like(ref)

    @pl.when((s == 0) | (s % n_tiles == 2 % n_tiles))
    def _():
        acc_ref[...] = jnp.zeros_like(acc_ref)

    def gated_tile(hid_r, gh_w, ii, cc, part):
        n_sub = PEER_N_KEYS // GATE_TILE_PARTS
        sub = slice(part * n_sub, (part + 1) * n_sub)
        rows = slice(ii * PEER_N_KEYS + part * n_sub, ii * PEER_N_KEYS + (part + 1) * n_sub)
        lanes = slice(cc * LANES, (cc + 1) * LANES)
        gate = jnp.zeros((n_sub, LANES), F32)
        for hh in range(PEER_HEADS):
            thr = tq_ref[cc, hh, ii:ii + 1, :]
            scale = e0_ref[cc, hh, ii:ii + 1, :]
            gate = gate + jnp.where(s1_ref[cc, hh, sub, :] >= thr, e1_ref[cc, hh, sub, :], 0.0) * scale
        gh_w[rows, lanes] = (gate * _gelu(hid_r[rows, lanes])).astype(BF16)

    def step(hid_w, hid_r, gh_w, gh_r):
        tiles = [(ii, cc) for ii in range(te // PEER_N_KEYS) for cc in range(tm // LANES)]
        n_m = EXPERT_M_PIECES
        n_k = len(tiles) // (2 * n_m)
        d = acc_ref.shape[0]
        tile = iter(tiles)
        for nh in range(2):
            lanes = slice(nh * (tm // 2), (nh + 1) * (tm // 2))
            words_t = slice(nh * (tm // 4), (nh + 1) * (tm // 4))
            for mh in range(n_m):
                rows_c = slice(mh * (d // n_m), (mh + 1) * (d // n_m))
                words_c = slice(mh * (d // n_m // 2), (mh + 1) * (d // n_m // 2))
                rows_a = slice(mh * (te // n_m), (mh + 1) * (te // n_m))
                words_a = slice(mh * (te // n_m // 2), (mh + 1) * (te // n_m // 2))
                up = down = None
                for kc in range(n_k):
                    ii, cc = next(tile)
                    kk_a = slice(kc * (d // n_k), (kc + 1) * (d // n_k))
                    part = _dot_nt(_unpack(down_ref[words_a, kk_a]), _unpack(hn_ref[words_t, kk_a]))
                    down = part if down is None else down + part
                    gated_tile(hid_r, gh_w, ii, cc, 0)
                    kk_c = slice(kc * (te // n_k), (kc + 1) * (te // n_k))
                    part = jnp.dot(_unpack(upt_ref[words_c, kk_c]), gh_r[kk_c, lanes],
                                   preferred_element_type=F32)
                    up = part if up is None else up + part
                    gated_tile(hid_r, gh_w, ii, cc, 1)
                hid_w[rows_a, lanes] = down
                acc_ref[rows_c, lanes] += up

    @pl.when(s % 2 == 0)
    def _():
        step(hid0_ref, hid1_ref, gh1_ref, gh0_ref)

    @pl.when(s % 2 == 1)
    def _():
        step(hid1_ref, hid0_ref, gh0_ref, gh1_ref)

    @pl.when((s >= 2) & ((s - 2) % n_tiles == n_tiles - 1))
    def _():
        o_ref[...] = acc_ref[...].T


def _experts(hn, down, up_t, tq, e0, s1, e1, tm=512, te=1024):
    t, d = 2 * hn.shape[0], hn.shape[1]
    nt, ne = t // tm, 2 * down.shape[0] // te
    nck = tm // LANES
    ni = te // PEER_N_KEYS
    assert ni % 8 == 0
    tok_a = lambda s: jnp.minimum(s // ne, nt - 1)
    tok_b = lambda s: jnp.minimum(jnp.maximum(s - 1, 0) // ne, nt - 1)
    tok_c = lambda s: jnp.minimum(jnp.maximum(s - 2, 0) // ne, nt - 1)
    exp_b = lambda s: jnp.maximum(s - 1, 0) % ne
    exp_c = lambda s: jnp.maximum(s - 2, 0) % ne
    row_block = pl.BlockSpec((nck, PEER_HEADS, ni, LANES), lambda s: (tok_b(s), 0, exp_b(s), 0))
    full_block = pl.BlockSpec((nck, PEER_HEADS, PEER_N_KEYS, LANES), lambda s: (tok_b(s), 0, 0, 0))
    return pl.pallas_call(
        functools.partial(_experts_kernel, n_tiles=ne),
        grid=(nt * ne + 2,),
        in_specs=[
            pl.BlockSpec((tm // 2, d), lambda s: (tok_a(s), 0)),
            pl.BlockSpec((te // 2, d), lambda s: (s % ne, 0)),
            pl.BlockSpec((d // 2, te), lambda s: (0, exp_c(s))),
            row_block, row_block, full_block, full_block,
        ],
        out_specs=pl.BlockSpec((tm, d), lambda s: (tok_c(s), 0)),
        out_shape=jax.ShapeDtypeStruct((t, d), F32),
        scratch_shapes=[
            pltpu.VMEM((d, tm), F32),
            pltpu.VMEM((te, tm), F32), pltpu.VMEM((te, tm), F32),
            pltpu.VMEM((te, tm), BF16), pltpu.VMEM((te, tm), BF16),
        ],
        compiler_params=_params(("arbitrary",)),
        name="peer_experts",
    )(hn, down, up_t, tq, e0, s1, e1)


def _ple_kernel(h_ref, y_ref, p_ref, gple_ref, wgate_ref, wproj_ref, gfin_ref, o_ref, *, final):
    h = h_ref[...] + y_ref[...]
    gate = jax.nn.sigmoid(jnp.dot(_rms(h, gple_ref[...]).astype(BF16), wgate_ref[...], preferred_element_type=F32))
    emb = jnp.dot(p_ref[...].astype(BF16), wproj_ref[...], preferred_element_type=F32)
    h = h + gate * emb
    o_ref[...] = _rms(h, gfin_ref[...]) if final else h


def _ple(h, y, p, g_ple, w_gate, w_proj, g_final, final, tm=512):
    t, d = h.shape
    pd = p.shape[1]
    return pl.pallas_call(
        functools.partial(_ple_kernel, final=final),
        grid=(t // tm,),
        in_specs=[
            pl.BlockSpec((tm, d), lambda i: (i, 0)),
            pl.BlockSpec((tm, d), lambda i: (i, 0)),
            pl.BlockSpec((tm, pd), lambda i: (i, 0)),
            _const_spec((1, d)),
            _const_spec(w_gate.shape),
            _const_spec(w_proj.shape),
            _const_spec((1, d)),
        ],
        out_specs=pl.BlockSpec((tm, d), lambda i: (i, 0)),
        out_shape=jax.ShapeDtypeStruct((t, d), F32),
        compiler_params=_params(("parallel",)),
        name="ple",
    )(h, y, p, g_ple, w_gate, w_proj, g_final)


def kernel(x, p, norm_mix_g, w_in, sgu_ln_g, sgu_ln_b, sgu_w, sgu_b, w_branch_attn, w_branch_sgu, w_out, norm_ffn_g, peer_w_query, peer_sub_keys, peer_down, peer_up, norm_ple_g, ple_w_proj, ple_w_gate, final_norm_g):
    b, s, d = x.shape
    assert d == D_MODEL
    t = b * s
    depth = w_in.shape[0]
    h = x.reshape(t, d)
    row = lambda v: v.reshape(1, -1)
    aw = ATT_WIDTH
    for i in range(depth):
        proj = _inproj(h, row(norm_mix_g[i]), w_in[i].astype(BF16), col_shift=3 * aw + 2 * SGU_WIDTH)
        yatt = _moba(proj.reshape(b, s, -1)).reshape(t, aw)
        bias_full = jnp.repeat(sgu_b[i].T, SGU_GROUP_DIM, axis=1)
        h, hn = _tail(proj, yatt, h, row(sgu_ln_g[i]), row(sgu_ln_b[i]), sgu_w[i], bias_full,
                      w_branch_attn[i].astype(BF16), w_branch_sgu[i].astype(BF16), w_out[i].astype(BF16),
                      row(norm_ffn_g[i]))
        keys = peer_sub_keys[i].reshape(2 * PEER_HEADS, PEER_N_KEYS, PEER_HALF).astype(BF16)
        tq, e0, s1, e1 = _route(hn, peer_w_query[i].T.astype(BF16), keys)
        y = _experts(hn, _pack_bf16(peer_down[i]), _pack_bf16(peer_up[i], transpose=True), tq, e0, s1, e1)
        h = _ple(h, y, p[i].reshape(t, -1), row(norm_ple_g[i]), ple_w_gate[i].astype(BF16),
                 ple_w_proj[i].astype(BF16), row(final_norm_g), final=(i == depth - 1))
    if depth == 0:
        raise ValueError("depth must be at least 1")
    return h.reshape(b, s, d)
```

```python
import functools

import jax
import jax.numpy as jnp
from jax import lax
from jax.experimental import pallas as pl
from jax.experimental.pallas import tpu as pltpu

F32 = jnp.float32
BF16 = jnp.bfloat16

NORM_EPS = 1e-6
ATT_HEADS = 8
ATT_HEAD_DIM = 128
ATT_WIDTH = ATT_HEADS * ATT_HEAD_DIM
MOBA_BLOCK = 256
MOBA_TOPK = 3
MOBA_BLOCKS_PER_STEP = 4
MOBA_HEADS_PER_STEP = 4
SGU_GROUPS = 8
SGU_GROUP_DIM = 128
SGU_WIDTH = SGU_GROUPS * SGU_GROUP_DIM
SGU_CHUNK = 128
PEER_HEADS = 8
PEER_N_KEYS = 128
PEER_HALF = 128
PEER_TOPK = 16

GATE_TILE_PARTS = 2
EXPERT_M_PIECES = 4
LANES = 128
SUBLANES = 8
BF16_ROWS = 2 * SUBLANES
VMEM_LIMIT = 56 * 1024 * 1024
MASK_NEG = float("-inf")
LOG2E = 1.4426950408889634

D_MODEL = 2048
COL_GATE_A = 0
COL_GATE_B = D_MODEL
COL_Q = 2 * D_MODEL
COL_K = COL_Q + ATT_WIDTH
COL_V = COL_K + ATT_WIDTH
COL_U = COL_V + ATT_WIDTH
COL_VG = COL_U + SGU_WIDTH


def _params(semantics):
    return pltpu.CompilerParams(dimension_semantics=semantics, vmem_limit_bytes=VMEM_LIMIT)


def _rms(x, gain):
    return x * lax.rsqrt(jnp.mean(x * x, axis=-1, keepdims=True) + NORM_EPS) * gain


def _gelu(x):
    return 0.5 * x * (1.0 + lax.erf(x * (0.5 ** 0.5)))


def _dot_nt(a, b):
    return lax.dot_general(a, b, (((1,), (1,)), ((), ())), preferred_element_type=F32)


def _unpack(words):
    return pltpu.bitcast(words, BF16)


def _pack_kernel(x_ref, o_ref, *, transpose):
    x = x_ref[...]
    if transpose:
        x = x.T
    o_ref[...] = pltpu.bitcast(x.astype(BF16), jnp.uint32)


def _pack_bf16(x, transpose=False, tr=512):
    r, c = x.shape
    if transpose:
        out_shape, out_block, out_map = (c // 2, r), (c // 2, tr), lambda i: (0, i)
    else:
        out_shape, out_block, out_map = (r // 2, c), (tr // 2, c), lambda i: (i, 0)
    return pl.pallas_call(
        functools.partial(_pack_kernel, transpose=transpose),
        grid=(r // tr,),
        in_specs=[pl.BlockSpec((tr, c), lambda i: (i, 0))],
        out_specs=pl.BlockSpec(out_block, out_map),
        out_shape=jax.ShapeDtypeStruct(out_shape, jnp.uint32),
        compiler_params=_params(("parallel",)),
        name="pack_bf16_t" if transpose else "pack_bf16",
    )(x)


def _inproj_kernel(x_ref, g_ref, w_ref, o_ref, xn_ref):
    @pl.when(pl.program_id(1) == 0)
    def _():
        xn_ref[...] = _rms(x_ref[...], g_ref[...]).astype(BF16)

    o_ref[...] = jnp.dot(xn_ref[...], w_ref[...], preferred_element_type=F32).astype(o_ref.dtype)


def _inproj(x, gain, w, col_shift, tm=1024, tn=1024):
    t, d = x.shape
    n = w.shape[1]
    assert col_shift % tn == 0
    shift = col_shift // tn
    return pl.pallas_call(
        _inproj_kernel,
        grid=(t // tm, n // tn),
        in_specs=[
            pl.BlockSpec((tm, d), lambda i, j: (i, 0)),
            pl.BlockSpec((1, d), lambda i, j: (0, 0)),
            pl.BlockSpec((d, tn), lambda i, j: (0, (j + shift) % (n // tn))),
        ],
        out_specs=pl.BlockSpec((tm, tn), lambda i, j: (i, j)),
        out_shape=jax.ShapeDtypeStruct((t, n), BF16),
        scratch_shapes=[pltpu.VMEM((tm, d), BF16)],
        compiler_params=_params(("parallel", "arbitrary")),
        name="inproj",
    )(x, gain, w)


def _moba_kernel(q_ref, k_ref, v_ref, o_ref, kmean_ref, vt_ref, mask_ref, score_ref):
    j = pl.program_id(2)
    blk, hd, grp = MOBA_BLOCK, ATT_HEAD_DIM, MOBA_BLOCKS_PER_STEP
    heads = range(q_ref.shape[1] // hd)
    nb = k_ref.shape[0] // blk
    nbp = kmean_ref.shape[1]
    ones_rows = vt_ref.shape[2] - hd
    c_exp = (hd ** -0.5) * LOG2E

    def col(h):
        return slice(h * hd, (h + 1) * hd)

    @pl.when(j == 0)
    def _():
        kmean_ref[...] = jnp.zeros_like(kmean_ref)

        def body(g, c):
            for u in range(grp):
                n = g * grp + u
                rows = pl.ds(pl.multiple_of(n * blk, blk), blk)
                for h in heads:
                    kmean_ref[h, pl.ds(n, 1), :] = (jnp.sum(k_ref[rows, col(h)].astype(F32), axis=0, keepdims=True)
                                                    * (1.0 / blk))
                    vt_ref[h, g, :hd, u * blk:(u + 1) * blk] = v_ref[rows, col(h)].T
            for h in heads:
                vt_ref[h, g, hd:, :] = jnp.ones((ones_rows, grp * blk), BF16)
            return c

        lax.fori_loop(0, nb // grp, body, 0)

    own = pl.ds(pl.multiple_of(j * blk, blk), blk)
    qs = [q_ref[:, col(h)] for h in heads]
    gates = [_dot_nt(kmean_ref[h].astype(BF16), qs[h]) for h in heads]
    t_own = [_dot_nt(k_ref[own, col(h)], qs[h]) * c_exp for h in heads]

    def group_scores(h, g):
        rows = pl.ds(pl.multiple_of(g * (grp * blk), grp * blk), grp * blk)
        return _dot_nt(k_ref[rows, col(h)], qs[h]) * c_exp

    row = lax.broadcasted_iota(jnp.int32, gates[0].shape, 0)
    for h in heads:
        g = jnp.where(row < j, gates[h], -jnp.inf)
        maskneg = jnp.full(g.shape, MASK_NEG, F32)
        for r in range(MOBA_TOPK):
            m = jnp.max(g, axis=0, keepdims=True)
            idx = jnp.min(jnp.where(g == m, row, nbp), axis=0, keepdims=True)
            idx = jnp.where(r < j, idx, -1)
            hit = row == idx
            maskneg = jnp.where(hit, 0.0, maskneg)
            g = jnp.where(hit, -jnp.inf, g)
        mask_ref[h] = maskneg
        score_ref[h] = group_scores(h, 0)

    krow = lax.broadcasted_iota(jnp.int32, (blk, blk), 0)
    qcol = lax.broadcasted_iota(jnp.int32, (blk, blk), 1)
    ones = jnp.ones((ones_rows, blk), BF16)
    state = []
    for h in heads:
        t = jnp.where(krow <= qcol, t_own[h], -jnp.inf)
        m0 = jnp.max(t, axis=0, keepdims=True)
        p = jnp.exp2(t - m0)
        v_own = jnp.concatenate([v_ref[own, col(h)].T, ones], axis=0)
        state.append((m0, jnp.dot(v_own, p.astype(BF16), preferred_element_type=F32)))

    def attend(g, carry):
        g_next = jnp.minimum(g + 1, nb // grp - 1)
        new = []
        for h in heads:
            m_prev, acc = carry[h]
            scores = score_ref[h]
            ts = [scores[u * blk:(u + 1) * blk] + mask_ref[h, pl.ds(g * grp + u, 1), :] for u in range(grp)]
            m_new = m_prev
            for t in ts:
                m_new = jnp.maximum(m_new, jnp.max(t, axis=0, keepdims=True))
            ps = [jnp.exp2(t - m_new).astype(BF16) for t in ts]
            score_ref[h] = group_scores(h, g_next)
            acc = (jnp.exp2(m_prev - m_new) * acc
                   + jnp.dot(vt_ref[h, g], jnp.concatenate(ps, axis=0), preferred_element_type=F32))
            new.append((m_new, acc))
        return tuple(new)

    final = lax.fori_loop(0, (j + grp - 1) // grp, attend, tuple(state))
    for h in heads:
        _, acc = final[h]
        o_ref[:, col(h)] = (acc[:hd] / acc[hd:hd + 1]).T.astype(o_ref.dtype)


def _moba(proj3):
    b, s, _ = proj3.shape
    blk, hd, grp, nh = MOBA_BLOCK, ATT_HEAD_DIM, MOBA_BLOCKS_PER_STEP, MOBA_HEADS_PER_STEP
    nb = s // blk
    assert s % blk == 0 and nb % grp == 0 and ATT_HEADS % nh == 0
    nbp = -(-nb // 16) * 16
    wide = nh * hd
    qb, kb, vb = COL_Q // wide, COL_K // wide, COL_V // wide
    return pl.pallas_call(
        _moba_kernel,
        grid=(b, ATT_HEADS // nh, nb),
        in_specs=[
            pl.BlockSpec((None, blk, wide), lambda bi, h, j: (bi, j, qb + h)),
            pl.BlockSpec((None, s, wide), lambda bi, h, j: (bi, 0, kb + h)),
            pl.BlockSpec((None, s, wide), lambda bi, h, j: (bi, 0, vb + h)),
        ],
        out_specs=pl.BlockSpec((None, blk, wide), lambda bi, h, j: (bi, j, h)),
        out_shape=jax.ShapeDtypeStruct((b, s, ATT_WIDTH), BF16),
        scratch_shapes=[
            pltpu.VMEM((nh, nbp, hd), F32),
            pltpu.VMEM((nh, nb // grp, hd + BF16_ROWS, grp * blk), BF16),
            pltpu.VMEM((nh, nbp, blk), F32),
            pltpu.VMEM((nh, grp * blk, blk), F32),
        ],
        compiler_params=_params(("parallel", "parallel", "arbitrary")),
        name="moba",
    )(proj3, proj3, proj3)


def _tail_kernel(ga_ref, gb_ref, u_ref, vg_ref, yatt_ref, x_ref, lng_ref, lnb_ref, ws_ref, bias_ref,
                 wba_ref, wbs_ref, wout_ref, gffn_ref, h_ref, hn_ref, ysgu_ref):
    tm = u_ref.shape[0]
    ch, gd = SGU_CHUNK, SGU_GROUP_DIM
    gu = _gelu(u_ref[...].astype(F32))
    gv = _gelu(vg_ref[...].astype(F32))
    mu = jnp.mean(gv, axis=-1, keepdims=True)
    xc = gv - mu
    vn = xc * lax.rsqrt(jnp.mean(xc * xc, axis=-1, keepdims=True) + NORM_EPS)
    vn = (vn * lng_ref[...] + lnb_ref[...]).astype(BF16)
    trow = lax.broadcasted_iota(jnp.int32, (ch, ch), 0)
    tcol = lax.broadcasted_iota(jnp.int32, (ch, ch), 1)
    for g in range(SGU_GROUPS):
        w = jnp.where(tcol <= trow, ws_ref[g], 0.0).astype(BF16)
        cols = slice(g * gd, (g + 1) * gd)
        for c in range(tm // ch):
            rows = slice(c * ch, (c + 1) * ch)
            mixed = jnp.dot(w, vn[rows, cols], preferred_element_type=F32) + bias_ref[:, cols]
            ysgu_ref[rows, cols] = (gu[rows, cols] * mixed).astype(BF16)
    a = jnp.dot(yatt_ref[...], wba_ref[...], preferred_element_type=F32)
    bsg = jnp.dot(ysgu_ref[...], wbs_ref[...], preferred_element_type=F32)
    merged = (jax.nn.sigmoid(ga_ref[...].astype(F32)) * a
              + jax.nn.sigmoid(gb_ref[...].astype(F32)) * bsg)
    h = x_ref[...] + jnp.dot(merged.astype(BF16), wout_ref[...], preferred_element_type=F32)
    h_ref[...] = h
    hn_ref[...] = pltpu.bitcast(_rms(h, gffn_ref[...]).astype(BF16), jnp.uint32)


def _const_spec(shape):
    return pl.BlockSpec(shape, lambda *_: (0,) * len(shape), pipeline_mode=pl.Buffered(1))


def _tail(proj, yatt, x, ln_g, ln_b, w_s, bias_full, w_ba, w_bs, w_out, g_ffn, tm=256):
    t, d = x.shape
    wide, narrow = d, SGU_WIDTH
    return pl.pallas_call(
        _tail_kernel,
        grid=(t // tm,),
        in_specs=[
            pl.BlockSpec((tm, wide), lambda i: (i, COL_GATE_A // wide)),
            pl.BlockSpec((tm, wide), lambda i: (i, COL_GATE_B // wide)),
            pl.BlockSpec((tm, narrow), lambda i: (i, COL_U // narrow)),
            pl.BlockSpec((tm, narrow), lambda i: (i, COL_VG // narrow)),
            pl.BlockSpec((tm, ATT_WIDTH), lambda i: (i, 0)),
            pl.BlockSpec((tm, d), lambda i: (i, 0)),
            _const_spec((1, narrow)),
            _const_spec((1, narrow)),
            _const_spec(w_s.shape),
            _const_spec(bias_full.shape),
            _const_spec(w_ba.shape),
            _const_spec(w_bs.shape),
            _const_spec(w_out.shape),
            _const_spec((1, d)),
        ],
        out_specs=[pl.BlockSpec((tm, d), lambda i: (i, 0)), pl.BlockSpec((tm // 2, d), lambda i: (i, 0))],
        out_shape=[jax.ShapeDtypeStruct((t, d), F32), jax.ShapeDtypeStruct((t // 2, d), jnp.uint32)],
        scratch_shapes=[pltpu.VMEM((tm, narrow), BF16)],
        compiler_params=_params(("parallel",)),
        name="mixer_tail",
    )(proj, proj, proj, proj, yatt, x, ln_g, ln_b, w_s, bias_full, w_ba, w_bs, w_out, g_ffn)


N_EXTRACT = PEER_TOPK + 1
CAND_PAIRS = tuple((a, b) for a in range(N_EXTRACT) for b in range(N_EXTRACT) if (a + 1) * (b + 1) <= N_EXTRACT)
CAND_ROWS = SUBLANES * (1 << (-(-len(CAND_PAIRS) // SUBLANES) - 1).bit_length())
ROUTE_HEADS_PER_STEP = 2


def _extract_top(works, n_iter):
    rows = works[0].shape[0]
    row = lax.broadcasted_iota(jnp.int32, works[0].shape, 0)
    works = list(works)
    vals = [[] for _ in works]
    for _ in range(n_iter):
        for k, work in enumerate(works):
            m = jnp.max(work, axis=0, keepdims=True)
            idx = jnp.min(jnp.where(work == m, row, rows), axis=0, keepdims=True)
            works[k] = jnp.where(row == idx, -jnp.inf, work)
            vals[k].append(m)
    return vals


def _sorting_network(n):
    pairs = []
    p = 1
    while p < n:
        k = p
        while k >= 1:
            for j in range(k % p, n - k, 2 * k):
                for i in range(min(k, n - j - k)):
                    if (i + j) // (2 * p) == (i + j + k) // (2 * p):
                        pairs.append((i + j, i + j + k))
            k //= 2
        p *= 2
    return pairs


def _extract_top_distinct(works, n_iter):
    lists = []
    for work in works:
        slabs = [work[SUBLANES * k:SUBLANES * (k + 1)] for k in range(work.shape[0] // SUBLANES)]
        for i, j in _sorting_network(len(slabs)):
            slabs[i], slabs[j] = jnp.maximum(slabs[i], slabs[j]), jnp.minimum(slabs[i], slabs[j])
        lists.append(slabs)
    vals = [[] for _ in works]
    popped = [jnp.zeros_like(slabs[0]) for slabs in lists]
    for r in range(n_iter):
        for k, slabs in enumerate(lists):
            m = jnp.max(slabs[0], axis=0, keepdims=True)
            hit = slabs[0] == m
            for d in range(min(len(slabs), n_iter - r - 1)):
                below = slabs[d + 1] if d + 1 < len(slabs) else -jnp.inf
                slabs[d] = jnp.where(hit, below, slabs[d])
            popped[k] = popped[k] + jnp.where(hit, 1.0, 0.0)
            vals[k].append(m)
    return [(v, jnp.abs(jnp.sum(p, axis=0, keepdims=True) - n_iter)) for v, p in zip(vals, popped)]


def _route_kernel(hn_ref, wqt_ref, keys_ref, tq_ref, e0_ref, s1_ref, e1_ref, cand_ref):
    hn = _unpack(hn_ref[...])
    tm = hn.shape[0]
    group = cand_ref.shape[0]
    cand_ref[...] = jnp.full(cand_ref.shape, -jnp.inf, F32)

    def candidates(k, tv0, tv1):
        for r, (a, b) in enumerate(CAND_PAIRS):
            cand_ref[k, r:r + 1, :] = tv0[a] + tv1[b]

    def finish(hh, s0, max0, c):
        z = jnp.ones_like(c[0])
        for r in range(1, PEER_TOPK):
            z = z + jnp.exp(c[r] - c[0])
        tau = 0.5 * (c[PEER_TOPK - 1] + c[PEER_TOPK])
        store(tq_ref, hh, tau - s0)
        store(e0_ref, hh, jnp.exp(s0 - max0) / z)

    def store(ref, hh, val):
        for cc in range(tm // LANES):
            ref[cc, hh] = val[:, cc * LANES:(cc + 1) * LANES]

    def heads(g, carry):
        scores = []
        for k in range(group):
            hh = g * group + k
            w = wqt_ref[pl.ds(pl.multiple_of(hh * 2 * PEER_HALF, 2 * PEER_HALF), 2 * PEER_HALF), :]
            qt = _dot_nt(w, hn).astype(BF16)
            scores.append((jnp.dot(keys_ref[2 * hh], qt[:PEER_HALF], preferred_element_type=F32),
                           jnp.dot(keys_ref[2 * hh + 1], qt[PEER_HALF:], preferred_element_type=F32)))
        tops = _extract_top_distinct([s for pair in scores for s in pair], N_EXTRACT)
        for k in range(group):
            candidates(k, tops[2 * k][0], tops[2 * k + 1][0])
        ctops = _extract_top_distinct([cand_ref[k] for k in range(group)], N_EXTRACT)
        for k, (s0, s1) in enumerate(scores):
            hh = g * group + k
            (tv0, bad0), (tv1, bad1), (c, badc) = tops[2 * k], tops[2 * k + 1], ctops[k]
            finish(hh, s0, tv0[0], c)
            store(s1_ref, hh, s1)
            store(e1_ref, hh, jnp.exp(s1 - tv1[0]))
            scores_tied = jnp.max(bad0 + bad1) > 0.0
            sums_tied = jnp.max(badc) > 0.0

            @pl.when(scores_tied)
            def _():
                xv0, xv1 = _extract_top([s0, s1], N_EXTRACT)
                candidates(k, xv0, xv1)
                finish(hh, s0, xv0[0], _extract_top([cand_ref[k]], N_EXTRACT)[0])

            @pl.when(jnp.logical_and(jnp.logical_not(scores_tied), sums_tied))
            def _():
                finish(hh, s0, tv0[0], _extract_top([cand_ref[k]], N_EXTRACT)[0])

        return carry

    lax.fori_loop(0, PEER_HEADS // group, heads, 0)


def _route(hn, wq_t, keys, tm=512):
    t, d = 2 * hn.shape[0], hn.shape[1]
    nck = tm // LANES
    out_block = pl.BlockSpec((nck, PEER_HEADS, PEER_N_KEYS, LANES), lambda i: (i, 0, 0, 0))
    out_shape = jax.ShapeDtypeStruct((t // LANES, PEER_HEADS, PEER_N_KEYS, LANES), F32)
    return pl.pallas_call(
        _route_kernel,
        grid=(t // tm,),
        in_specs=[
            pl.BlockSpec((tm // 2, d), lambda i: (i, 0)),
            _const_spec(wq_t.shape),
            _const_spec(keys.shape),
        ],
        out_specs=[out_block] * 4,
        out_shape=[out_shape] * 4,
        scratch_shapes=[pltpu.VMEM((ROUTE_HEADS_PER_STEP, CAND_ROWS, tm), F32)],
        compiler_params=_params(("parallel",)),
        name="peer_route",
    )(hn, wq_t, keys)


def _experts_kernel(hn_ref, down_ref, upt_ref, tq_ref, e0_ref, s1_ref, e1_ref, o_ref,
                    acc_ref, hid0_ref, hid1_ref, gh0_ref, gh1_ref, *, n_tiles):
    s = pl.program_id(0)
    te = 2 * down_ref.shape[0]
    tm = 2 * hn_ref.shape[0]

    @pl.when(s == 0)
    def _():
        for ref in (hid0_ref, hid1_ref, gh0_ref, gh1_ref):
            ref[...] = jnp.zeros_like(ref)

    @pl.when((s == 0) | (s % n_tiles == 2 % n_tiles))
    def _():
        acc_ref[...] = jnp.zeros_like(acc_ref)

    def gated_tile(hid_r, gh_w, ii, cc, part):
        n_sub = PEER_N_KEYS // GATE_TILE_PARTS
        sub = slice(part * n_sub, (part + 1) * n_sub)
        rows = slice(ii * PEER_N_KEYS + part * n_sub, ii * PEER_N_KEYS + (part + 1) * n_sub)
        lanes = slice(cc * LANES, (cc + 1) * LANES)
        gate = jnp.zeros((n_sub, LANES), F32)
        for hh in range(PEER_HEADS):
            thr = tq_ref[cc, hh, ii:ii + 1, :]
            scale = e0_ref[cc, hh, ii:ii + 1, :]
            gate = gate + jnp.where(s1_ref[cc, hh, sub, :] >= thr, e1_ref[cc, hh, sub, :], 0.0) * scale
        gh_w[rows, lanes] = (gate * _gelu(hid_r[rows, lanes])).astype(BF16)

    def step(hid_w, hid_r, gh_w, gh_r):
        tiles = [(ii, cc) for ii in range(te // PEER_N_KEYS) for cc in range(tm // LANES)]
        n_m = EXPERT_M_PIECES
        n_k = len(tiles) // (2 * n_m)
        d = acc_ref.shape[0]
        tile = iter(tiles)
        for nh in range(2):
            lanes = slice(nh * (tm // 2), (nh + 1) * (tm // 2))
            words_t = slice(nh * (tm // 4), (nh + 1) * (tm // 4))
            for mh in range(n_m):
                rows_c = slice(mh * (d // n_m), (mh + 1) * (d // n_m))
                words_c = slice(mh * (d // n_m // 2), (mh + 1) * (d // n_m // 2))
                rows_a = slice(mh * (te // n_m), (mh + 1) * (te // n_m))
                words_a = slice(mh * (te // n_m // 2), (mh + 1) * (te // n_m // 2))
                up = down = None
                for kc in range(n_k):
                    ii, cc = next(tile)
                    kk_a = slice(kc * (d // n_k), (kc + 1) * (d // n_k))
                    part = _dot_nt(_unpack(down_ref[words_a, kk_a]), _unpack(hn_ref[words_t, kk_a]))
                    down = part if down is None else down + part
                    gated_tile(hid_r, gh_w, ii, cc, 0)
                    kk_c = slice(kc * (te // n_k), (kc + 1) * (te // n_k))
                    part = jnp.dot(_unpack(upt_ref[words_c, kk_c]), gh_r[kk_c, lanes],
                                   preferred_element_type=F32)
                    up = part if up is None else up + part
                    gated_tile(hid_r, gh_w, ii, cc, 1)
                hid_w[rows_a, lanes] = down
                acc_ref[rows_c, lanes] += up

    @pl.when(s % 2 == 0)
    def _():
        step(hid0_ref, hid1_ref, gh1_ref, gh0_ref)

    @pl.when(s % 2 == 1)
    def _():
        step(hid1_ref, hid0_ref, gh0_ref, gh1_ref)

    @pl.when((s >= 2) & ((s - 2) % n_tiles == n_tiles - 1))
    def _():
        o_ref[...] = acc_ref[...].T


def _experts(hn, down, up_t, tq, e0, s1, e1, tm=512, te=1024):
    t, d = 2 * hn.shape[0], hn.shape[1]
    nt, ne = t // tm, 2 * down.shape[0] // te
    nck = tm // LANES
    ni = te // PEER_N_KEYS
    assert ni % 8 == 0
    tok_a = lambda s: jnp.minimum(s // ne, nt - 1)
    tok_b = lambda s: jnp.minimum(jnp.maximum(s - 1, 0) // ne, nt - 1)
    tok_c = lambda s: jnp.minimum(jnp.maximum(s - 2, 0) // ne, nt - 1)
    exp_b = lambda s: jnp.maximum(s - 1, 0) % ne
    exp_c = lambda s: jnp.maximum(s - 2, 0) % ne
    row_block = pl.BlockSpec((nck, PEER_HEADS, ni, LANES), lambda s: (tok_b(s), 0, exp_b(s), 0))
    full_block = pl.BlockSpec((nck, PEER_HEADS, PEER_N_KEYS, LANES), lambda s: (tok_b(s), 0, 0, 0))
    return pl.pallas_call(
        functools.partial(_experts_kernel, n_tiles=ne),
        grid=(nt * ne + 2,),
        in_specs=[
            pl.BlockSpec((tm // 2, d), lambda s: (tok_a(s), 0)),
            pl.BlockSpec((te // 2, d), lambda s: (s % ne, 0)),
            pl.BlockSpec((d // 2, te), lambda s: (0, exp_c(s))),
            row_block, row_block, full_block, full_block,
        ],
        out_specs=pl.BlockSpec((tm, d), lambda s: (tok_c(s), 0)),
        out_shape=jax.ShapeDtypeStruct((t, d), F32),
        scratch_shapes=[
            pltpu.VMEM((d, tm), F32),
            pltpu.VMEM((te, tm), F32), pltpu.VMEM((te, tm), F32),
            pltpu.VMEM((te, tm), BF16), pltpu.VMEM((te, tm), BF16),
        ],
        compiler_params=_params(("arbitrary",)),
        name="peer_experts",
    )(hn, down, up_t, tq, e0, s1, e1)


def _ple_kernel(h_ref, y_ref, p_ref, gple_ref, wgate_ref, wproj_ref, gfin_ref, o_ref, *, final):
    h = h_ref[...] + y_ref[...]
    gate = jax.nn.sigmoid(jnp.dot(_rms(h, gple_ref[...]).astype(BF16), wgate_ref[...], preferred_element_type=F32))
    emb = jnp.dot(p_ref[...].astype(BF16), wproj_ref[...], preferred_element_type=F32)
    h = h + gate * emb
    o_ref[...] = _rms(h, gfin_ref[...]) if final else h


def _ple(h, y, p, g_ple, w_gate, w_proj, g_final, final, tm=512):
    t, d = h.shape
    pd = p.shape[1]
    return pl.pallas_call(
        functools.partial(_ple_kernel, final=final),
        grid=(t // tm,),
        in_specs=[
            pl.BlockSpec((tm, d), lambda i: (i, 0)),
            pl.BlockSpec((tm, d), lambda i: (i, 0)),
            pl.BlockSpec((tm, pd), lambda i: (i, 0)),
            _const_spec((1, d)),
            _const_spec(w_gate.shape),
            _const_spec(w_proj.shape),
            _const_spec((1, d)),
        ],
        out_specs=pl.BlockSpec((tm, d), lambda i: (i, 0)),
        out_shape=jax.ShapeDtypeStruct((t, d), F32),
        compiler_params=_params(("parallel",)),
        name="ple",
    )(h, y, p, g_ple, w_gate, w_proj, g_final)


def kernel(x, p, norm_mix_g, w_in, sgu_ln_g, sgu_ln_b, sgu_w, sgu_b, w_branch_attn, w_branch_sgu, w_out, norm_ffn_g, peer_w_query, peer_sub_keys, peer_down, peer_up, norm_ple_g, ple_w_proj, ple_w_gate, final_norm_g):
    b, s, d = x.shape
    assert d == D_MODEL
    t = b * s
    depth = w_in.shape[0]
    h = x.reshape(t, d)
    row = lambda v: v.reshape(1, -1)
    aw = ATT_WIDTH
    for i in range(depth):
        proj = _inproj(h, row(norm_mix_g[i]), w_in[i].astype(BF16), col_shift=3 * aw + 2 * SGU_WIDTH)
        yatt = _moba(proj.reshape(b, s, -1)).reshape(t, aw)
        bias_full = jnp.repeat(sgu_b[i].T, SGU_GROUP_DIM, axis=1)
        h, hn = _tail(proj, yatt, h, row(sgu_ln_g[i]), row(sgu_ln_b[i]), sgu_w[i], bias_full,
                      w_branch_attn[i].astype(BF16), w_branch_sgu[i].astype(BF16), w_out[i].astype(BF16),
                      row(norm_ffn_g[i]))
        keys = peer_sub_keys[i].reshape(2 * PEER_HEADS, PEER_N_KEYS, PEER_HALF).astype(BF16)
        tq, e0, s1, e1 = _route(hn, peer_w_query[i].T.astype(BF16), keys)
        y = _experts(hn, _pack_bf16(peer_down[i]), _pack_bf16(peer_up[i], transpose=True), tq, e0, s1, e1)
        h = _ple(h, y, p[i].reshape(t, -1), row(norm_ple_g[i]), ple_w_gate[i].astype(BF16),
                 ple_w_proj[i].astype(BF16), row(final_norm_g), final=(i == depth - 1))
    if depth == 0:
        raise ValueError("depth must be at least 1")
    return h.reshape(b, s, d)
```
